```python
import jax, jax.numpy as jnp
from jax import lax
import numpy as np

D_MODEL = 1024
BATCH = 16
SEQ = 4096
DEPTH = 1

RET_HEADS = 8
RET_HEAD_DIM = 64
RET_WIDTH = RET_HEADS * RET_HEAD_DIM
RET_CHUNK = 128
MLA_HEADS = 8
MLA_Q_RANK = 256
MLA_KV_RANK = 128
MLA_NOPE_DIM = 64
MLA_ROPE_DIM = 32
MLA_QK_DIM = MLA_NOPE_DIM + MLA_ROPE_DIM
MLA_V_DIM = 64
MLA_WIDTH = MLA_HEADS * MLA_V_DIM
Q_BLOCK = 128
MIX_WIDTH = RET_WIDTH + MLA_WIDTH
IN_WIDTH = 4 * RET_WIDTH + MLA_Q_RANK + MLA_KV_RANK + MLA_ROPE_DIM
D_FF = -(-8 * D_MODEL // (3 * 256)) * 256
ROPE_BASE = 10000.0
EPS = 1e-6
GN_EPS = 1e-5

kernel_name = 'hybrid_retention_mla_encoder_block'


def rms_norm(x, g, eps=EPS):
    xf = x.astype(jnp.float32)
    y = xf * lax.rsqrt(jnp.mean(xf * xf, axis=-1, keepdims=True) + eps)
    return (y * g.astype(jnp.float32)).astype(x.dtype)


def rotary(x, positions):
    half = x.shape[-1] // 2
    inv_freq = ROPE_BASE ** (-jnp.arange(half, dtype=jnp.float32) / half)
    ang = positions.astype(jnp.float32)[..., None] * inv_freq
    cos = jnp.cos(ang)[:, :, None, :]
    sin = jnp.sin(ang)[:, :, None, :]
    xf = x.astype(jnp.float32)
    x1, x2 = xf[..., :half], xf[..., half:]
    out = jnp.concatenate([x1 * cos - x2 * sin, x2 * cos + x1 * sin], axis=-1)
    return out.astype(x.dtype)


def retention_one_direction(q, k, v, log_gamma, strict):
    C = q.shape[3]
    idx = jnp.arange(C, dtype=jnp.float32)
    diff = idx[:, None] - idx[None, :]
    mask = (diff > 0) if strict else (diff >= 0)
    lg = log_gamma[:, None, None]
    decay = jnp.where(mask, jnp.exp(lg * jnp.where(mask, diff, 0.0)), 0.0)
    scores = jnp.einsum('bhncd,bhnmd->bhncm', q, k) * decay[None, :, None]
    inner = jnp.einsum('bhncm,bhnmd->bhncd', scores, v)
    zeta = jnp.exp(log_gamma[:, None] * (C - 1 - idx)[None, :])
    xi = jnp.exp(log_gamma[:, None] * (idx + 1)[None, :])
    chunk_kv = jnp.einsum('bhncd,bhnce->nbhde', k * zeta[None, :, None, :, None], v)
    chunk_decay = jnp.exp(log_gamma * C)[None, :, None, None]

    def step(state, kv):
        return state * chunk_decay + kv, state

    _, prev = lax.scan(step, jnp.zeros_like(chunk_kv[0]), chunk_kv)
    cross = jnp.einsum('bhncd,nbhde->bhnce', q, prev) * xi[None, :, None, :, None]
    return inner + cross


def retention_mixer(q, k, v, g, positions, logit_fwd, logit_bwd):
    B, S, _ = q.shape
    n_chunks = S // RET_CHUNK

    def heads(t):
        return t.reshape(B, S, RET_HEADS, RET_HEAD_DIM)

    qh = rotary(heads(q), positions).astype(jnp.float32)
    kh = rotary(heads(k), positions).astype(jnp.float32) * (RET_HEAD_DIM ** -0.5)
    vh = heads(v).astype(jnp.float32)

    def chunked(t):
        return t.transpose(0, 2, 1, 3).reshape(B, RET_HEADS, n_chunks, RET_CHUNK, RET_HEAD_DIM)

    lg_f = jax.nn.log_sigmoid(logit_fwd.astype(jnp.float32))
    lg_b = jax.nn.log_sigmoid(logit_bwd.astype(jnp.float32))
    o_f = retention_one_direction(chunked(qh), chunked(kh), chunked(vh), lg_f, strict=False)
    o_b = retention_one_direction(chunked(jnp.flip(qh, 1)), chunked(jnp.flip(kh, 1)),
                                  chunked(jnp.flip(vh, 1)), lg_b, strict=True)
    o = (o_f.reshape(B, RET_HEADS, S, RET_HEAD_DIM)
         + jnp.flip(o_b.reshape(B, RET_HEADS, S, RET_HEAD_DIM), axis=2))
    mu = jnp.mean(o, axis=-1, keepdims=True)
    var = jnp.mean(jnp.square(o - mu), axis=-1, keepdims=True)
    o = (o - mu) * lax.rsqrt(var + GN_EPS)
    o = o.transpose(0, 2, 1, 3).reshape(B, S, RET_WIDTH)
    return (jax.nn.silu(g.astype(jnp.float32)) * o).astype(g.dtype)


def mla_mixer(c_q, c_kv, k_rope, positions, q_a_norm_g, w_uq, kv_a_norm_g, w_ukv, q_norm_g, k_norm_g):
    B, S, _ = c_q.shape
    q = (rms_norm(c_q, q_a_norm_g) @ w_uq).reshape(B, S, MLA_HEADS, MLA_QK_DIM)
    kv = (rms_norm(c_kv, kv_a_norm_g) @ w_ukv).reshape(B, S, MLA_HEADS, MLA_NOPE_DIM + MLA_V_DIM)
    k_nope, v = kv[..., :MLA_NOPE_DIM], kv[..., MLA_NOPE_DIM:]
    k_pe = jnp.broadcast_to(k_rope[:, :, None, :], (B, S, MLA_HEADS, MLA_ROPE_DIM))
    k = jnp.concatenate([k_nope, k_pe], axis=-1)
    q = rms_norm(q, q_norm_g)
    k = rms_norm(k, k_norm_g)
    q = jnp.concatenate([q[..., :MLA_NOPE_DIM], rotary(q[..., MLA_NOPE_DIM:], positions)], axis=-1)
    k = jnp.concatenate([k[..., :MLA_NOPE_DIM], rotary(k[..., MLA_NOPE_DIM:], positions)], axis=-1)
    scale = MLA_QK_DIM ** -0.5
    n_blocks = S // Q_BLOCK
    q_blocks = q.reshape(B, n_blocks, Q_BLOCK, MLA_HEADS, MLA_QK_DIM).transpose(1, 0, 2, 3, 4)

    def attend(qb):
        s = jnp.einsum('bqhd,bkhd->bhqk', qb, k).astype(jnp.float32) * scale
        p = jax.nn.softmax(s, axis=-1)
        return jnp.einsum('bhqk,bkhd->bqhd', p.astype(v.dtype), v)

    o = lax.map(attend, q_blocks)
    return o.transpose(1, 0, 2, 3, 4).reshape(B, S, MLA_WIDTH)


def swiglu(h, w_gate, w_up, w_down):
    a = jax.nn.silu((h @ w_gate).astype(jnp.float32)).astype(h.dtype)
    return (a * (h @ w_up)) @ w_down


def setup_inputs(seed: int = 0) -> dict:
    key = jax.random.key(seed)
    ks = jax.random.split(key, 20)
    f32 = jnp.float32

    def w(k, shape, fan_in):
        return jax.random.normal(k, shape, f32) * (fan_in ** -0.5)

    def gain(k, shape):
        return jnp.ones(shape, f32) + 0.01 * jax.random.normal(k, shape, f32)

    base_logit = jnp.asarray(np.log(2.0 ** (5 + np.arange(RET_HEADS)) - 1.0), dtype=f32)
    x = jax.random.normal(ks[0], (BATCH, SEQ, D_MODEL), f32)
    offset = jax.random.randint(ks[1], (BATCH, 1), 0, 1024, dtype=jnp.int32)
    positions = (jnp.arange(SEQ, dtype=jnp.int32)[None, :] + offset).astype(jnp.int32)
    return {
        'x': x,
        'positions': positions,
        'norm1_g': gain(ks[2], (DEPTH, D_MODEL)),
        'w_in': w(ks[3], (DEPTH, D_MODEL, IN_WIDTH), D_MODEL),
        'ret_decay_logit_fwd': base_logit[None] + 0.1 * jax.random.normal(ks[4], (DEPTH, RET_HEADS), f32),
        'ret_decay_logit_bwd': base_logit[None] + 0.1 * jax.random.normal(ks[5], (DEPTH, RET_HEADS), f32),
        'q_a_norm_g': gain(ks[6], (DEPTH, MLA_Q_RANK)),
        'w_uq': w(ks[7], (DEPTH, MLA_Q_RANK, MLA_HEADS * MLA_QK_DIM), MLA_Q_RANK),
        'kv_a_norm_g': gain(ks[8], (DEPTH, MLA_KV_RANK)),
        'w_ukv': w(ks[9], (DEPTH, MLA_KV_RANK, MLA_HEADS * (MLA_NOPE_DIM + MLA_V_DIM)), MLA_KV_RANK),
        'q_norm_g': gain(ks[10], (DEPTH, MLA_QK_DIM)),
        'k_norm_g': gain(ks[11], (DEPTH, MLA_QK_DIM)),
        'w_o': w(ks[12], (DEPTH, MIX_WIDTH, D_MODEL), MIX_WIDTH),
        'norm2_g': gain(ks[13], (DEPTH, D_MODEL)),
        'w_gate': w(ks[14], (DEPTH, D_MODEL, D_FF), D_MODEL),
        'w_up': w(ks[15], (DEPTH, D_MODEL, D_FF), D_MODEL),
        'w_down': w(ks[16], (DEPTH, D_FF, D_MODEL), D_FF),
    }


def reference(x, positions, norm1_g, w_in, ret_decay_logit_fwd, ret_decay_logit_bwd, q_a_norm_g, w_uq,
              kv_a_norm_g, w_ukv, q_norm_g, k_norm_g, w_o, norm2_g, w_gate, w_up, w_down):
    split_points = [RET_WIDTH, 2 * RET_WIDTH, 3 * RET_WIDTH, 4 * RET_WIDTH,
                    4 * RET_WIDTH + MLA_Q_RANK, 4 * RET_WIDTH + MLA_Q_RANK + MLA_KV_RANK]
    for layer in range(DEPTH):
        h = rms_norm(x, norm1_g[layer])
        proj = h @ w_in[layer]
        q_r, k_r, v_r, g_r, c_q, c_kv, k_rope = jnp.split(proj, split_points, axis=-1)
        y_ret = retention_mixer(q_r, k_r, v_r, g_r, positions,
                                ret_decay_logit_fwd[layer], ret_decay_logit_bwd[layer])
        y_mla = mla_mixer(c_q, c_kv, k_rope, positions, q_a_norm_g[layer], w_uq[layer],
                          kv_a_norm_g[layer], w_ukv[layer], q_norm_g[layer], k_norm_g[layer])
        x = x + jnp.concatenate([y_ret, y_mla], axis=-1) @ w_o[layer]
        x = x + swiglu(rms_norm(x, norm2_g[layer]), w_gate[layer], w_up[layer], w_down[layer])
    return x
```

```python
import functools
import math

import jax
import jax.numpy as jnp
from jax import lax
from jax.experimental import pallas as pl
from jax.experimental.pallas import tpu as pltpu

RET_HEADS = 8
RET_HEAD_DIM = 64
RET_WIDTH = RET_HEADS * RET_HEAD_DIM
RET_CHUNK = 128
MLA_HEADS = 8
MLA_Q_RANK = 256
MLA_KV_RANK = 128
MLA_NOPE_DIM = 64
MLA_ROPE_DIM = 32
MLA_QK_DIM = MLA_NOPE_DIM + MLA_ROPE_DIM
MLA_V_DIM = 64
MLA_WIDTH = MLA_HEADS * MLA_V_DIM
ROPE_BASE = 10000.0
EPS = 1e-6
GN_EPS = 1e-5

LANES = 128
HEAD_PAD = LANES
V_ROWS = 80
VMEM_LIMIT = 56 * 1024 * 1024

F32 = jnp.float32
BF16 = jnp.bfloat16


def _dot(a, b):
    return jnp.dot(a, b, preferred_element_type=F32)


def _dot_nt(a, b):
    return lax.dot_general(a, b, (((1,), (1,)), ((), ())), preferred_element_type=F32)


def _dot_tn(a, b):
    return lax.dot_general(a, b, (((0,), (0,)), ((), ())), preferred_element_type=F32)


def _rms(x, g, n):
    ms = jnp.sum(x * x, axis=-1, keepdims=True) * (1.0 / n)
    return x * lax.rsqrt(ms + EPS) * g


def _sigmoid(x):
    return 1.0 / (1.0 + jnp.exp(-x))


def _hi_lo(x):
    hi = x.astype(BF16)
    lo = (x - hi.astype(F32)).astype(BF16)
    return hi, lo


def _rope_tables_kernel(pos_ref, freq_ref, phase_ref, expand_ref, bias_ref, out_ref):
    ang = pos_ref[0] * freq_ref[...] + phase_ref[...]
    tab = jnp.sin(ang)
    hi, lo = _hi_lo(tab)
    e = expand_ref[...]
    out_ref[0] = _dot(hi, e) + _dot(lo, e) + bias_ref[...]


def _rope_tables(pos, freq, phase, expand, bias, tm):
    B, S, _ = pos.shape
    W = expand.shape[1]
    return pl.pallas_call(
        _rope_tables_kernel,
        out_shape=jax.ShapeDtypeStruct((B, S, W), F32),
        grid=(B, S // tm),
        in_specs=[
            pl.BlockSpec((1, tm, 1), lambda b, s: (b, s, 0)),
            pl.BlockSpec((1, LANES), lambda b, s: (0, 0)),
            pl.BlockSpec((1, LANES), lambda b, s: (0, 0)),
            pl.BlockSpec((LANES, W), lambda b, s: (0, 0)),
            pl.BlockSpec((1, W), lambda b, s: (0, 0)),
        ],
        out_specs=pl.BlockSpec((1, tm, W), lambda b, s: (b, s, 0)),
        compiler_params=pltpu.CompilerParams(
            dimension_semantics=("parallel", "parallel"), vmem_limit_bytes=VMEM_LIMIT),
        name="rope_tables",
    )(pos, freq, phase, expand, bias)


def _in_proj_kernel(x_ref, g_ref, w_ref, ret_ref, mla_ref):
    x = x_ref[0]
    h = _rms(x, g_ref[...], x.shape[-1]).astype(BF16)
    proj = _dot(h, w_ref[...])
    nret = ret_ref.shape[-1]
    ret_ref[0] = proj[:, :nret].astype(BF16)
    mla_ref[0] = proj[:, nret:].astype(BF16)


def _in_proj(x, g, w, tm):
    B, S, D = x.shape
    N = w.shape[1]
    nret = 4 * RET_WIDTH
    return pl.pallas_call(
        _in_proj_kernel,
        out_shape=(jax.ShapeDtypeStruct((B, S, nret), BF16),
                   jax.ShapeDtypeStruct((B, S, N - nret), BF16)),
        grid=(B, S // tm),
        in_specs=[
            pl.BlockSpec((1, tm, D), lambda b, s: (b, s, 0)),
            pl.BlockSpec((1, D), lambda b, s: (0, 0)),
            pl.BlockSpec((D, N), lambda b, s: (0, 0)),
        ],
        out_specs=(pl.BlockSpec((1, tm, nret), lambda b, s: (b, s, 0)),
                   pl.BlockSpec((1, tm, N - nret), lambda b, s: (b, s, 0))),
        compiler_params=pltpu.CompilerParams(
            dimension_semantics=("parallel", "parallel"), vmem_limit_bytes=VMEM_LIMIT),
        name="in_proj",
    )(x, g, w)


def _log_sigmoid(x):
    return jnp.minimum(x, 0.0) - jnp.log(1.0 + jnp.exp(-jnp.abs(x)))


def _retention_kernel(q_ref, k_ref, v_ref, g_ref, cos_ref, sin_ref,
                      lfl_ref, lbl_ref, lfc_ref, lbc_ref, o_ref, qs, ks, rb):
    S = q_ref.shape[1]
    C = RET_CHUNK
    n_chunks = S // C
    HD = RET_HEAD_DIM

    lane = lax.broadcasted_iota(jnp.int32, (1, LANES), 1)
    first_half = (lane % HD) < (HD // 2)
    head_a = lane < HD

    def rotate(x):
        swapped = jnp.where(first_half, pltpu.roll(x, LANES - HD // 2, 1), pltpu.roll(x, HD // 2, 1))
        return x * cos_ref[0] + swapped * sin_ref[0]

    qs[...] = rotate(q_ref[0].astype(F32)).astype(BF16)
    ks[...] = (rotate(k_ref[0].astype(F32)) * (HD ** -0.5)).astype(BF16)

    lgf_l = _log_sigmoid(lfl_ref[0])
    lgb_l = _log_sigmoid(lbl_ref[0])
    lgf_c = _log_sigmoid(lfc_ref[0])
    lgb_c = _log_sigmoid(lbc_ref[0])

    row = lax.broadcasted_iota(jnp.int32, (C, 2 * C), 0)
    col = lax.broadcasted_iota(jnp.int32, (C, 2 * C), 1) % C
    diff = (row - col).astype(F32)
    decay = jnp.where(diff >= 0, jnp.exp(lgf_c * jnp.maximum(diff, 0.0)),
                      jnp.exp(lgb_c * jnp.maximum(-diff, 0.0)))

    idx = lax.broadcasted_iota(jnp.int32, (C, LANES), 0).astype(F32)
    zeta_f = jnp.exp(lgf_l * (C - 1 - idx))
    xi_f = jnp.exp(lgf_l * (idx + 1))
    zeta_b = jnp.exp(lgb_l * idx)
    xi_b = jnp.exp(lgb_l * (C - idx))
    gc_f = jnp.exp(lgf_l * C)
    gc_b = jnp.exp(lgb_l * C)

    r2 = lax.broadcasted_iota(jnp.int32, (LANES, LANES), 0)
    c2 = lax.broadcasted_iota(jnp.int32, (LANES, LANES), 1)
    same_head = (r2 < HD) == (c2 < HD)
    avg = jnp.where(same_head, 1.0 / HD, 0.0).astype(BF16)

    def chunk_kv(kc, vc, zeta):
        vz = (vc.astype(F32) * zeta).astype(BF16)
        return jnp.where(same_head, _dot_tn(kc, vz), 0.0)

    def back_body(j, state):
        i = n_chunks - 1 - j
        sl = pl.ds(pl.multiple_of(i * C, C), C)
        rb[i] = state.astype(BF16)
        return state * gc_b + chunk_kv(ks[sl, :], v_ref[0, sl, :], zeta_b)

    lax.fori_loop(0, n_chunks, back_body, jnp.zeros((LANES, LANES), F32))

    zero = jnp.zeros((), BF16)

    def fwd_body(i, state):
        sl = pl.ds(pl.multiple_of(i * C, C), C)
        qc = qs[sl, :]
        kc = ks[sl, :]
        vc = v_ref[0, sl, :]
        k2 = jnp.concatenate([jnp.where(head_a, kc, zero), jnp.where(head_a, zero, kc)], axis=0)
        v2 = jnp.concatenate([jnp.where(head_a, vc, zero), jnp.where(head_a, zero, vc)], axis=0)
        p = (_dot_nt(qc, k2) * decay).astype(BF16)
        inner = _dot(p, v2)
        qf = qc.astype(F32)
        qx = jnp.concatenate([qf * xi_f, qf * xi_b], axis=1).astype(BF16)
        states = jnp.concatenate([state.astype(BF16), rb[i]], axis=0)
        o = inner + _dot(qx, states)
        o_hi, o_lo = _hi_lo(o)
        mu = _dot(o_hi, avg) + _dot(o_lo, avg)
        d = o - mu
        d2_hi, d2_lo = _hi_lo(d * d)
        var = _dot(d2_hi, avg) + _dot(d2_lo, avg)
        on = d * lax.rsqrt(var + GN_EPS)
        gate = g_ref[0, sl, :].astype(F32)
        o_ref[0, sl, :] = (gate * _sigmoid(gate) * on).astype(o_ref.dtype)
        return state * gc_f + chunk_kv(kc, vc, zeta_f)

    lax.fori_loop(0, n_chunks, fwd_body, jnp.zeros((LANES, LANES), F32))


def _retention(ret, tables, lfl, lbl, lfc, lbc):
    B, S, _ = ret.shape
    n_pairs = RET_WIDTH // LANES
    n_chunks = S // RET_CHUNK

    def col(off):
        return pl.BlockSpec((1, S, LANES), lambda b, hp: (b, 0, off + hp))

    def per_pair(width):
        return pl.BlockSpec((1, 1, width), lambda b, hp: (hp, 0, 0))

    return pl.pallas_call(
        _retention_kernel,
        out_shape=jax.ShapeDtypeStruct((B, S, RET_WIDTH), BF16),
        grid=(B, n_pairs),
        in_specs=[
            col(0), col(n_pairs), col(2 * n_pairs), col(3 * n_pairs),
            pl.BlockSpec((1, S, LANES), lambda b, hp: (b, 0, 0)),
            pl.BlockSpec((1, S, LANES), lambda b, hp: (b, 0, 1)),
            per_pair(LANES), per_pair(LANES), per_pair(2 * RET_CHUNK), per_pair(2 * RET_CHUNK),
        ],
        out_specs=pl.BlockSpec((1, S, LANES), lambda b, hp: (b, 0, hp)),
        scratch_shapes=[
            pltpu.VMEM((S, LANES), BF16),
            pltpu.VMEM((S, LANES), BF16),
            pltpu.VMEM((n_chunks, LANES, LANES), BF16),
        ],
        compiler_params=pltpu.CompilerParams(
            dimension_semantics=("parallel", "parallel"), vmem_limit_bytes=VMEM_LIMIT),
        name="retention",
    )(ret, ret, ret, ret, tables, tables, lfl, lbl, lfc, lbc)


def _mla_prep_kernel(mla_ref, cos_ref, sin_ref, gqa_ref, wq_ref, gkva_ref, wkv_ref,
                     gq_ref, gqs_ref, gk_ref, gks_ref, qt_ref, k_ref, vt_ref):
    H = MLA_HEADS
    tm = mla_ref.shape[1]
    cosm = cos_ref[0]
    sinm = sin_ref[0]
    m = mla_ref[0].astype(F32)
    cq = m[:, :MLA_Q_RANK]
    ckv = m[:, MLA_Q_RANK:MLA_Q_RANK + MLA_KV_RANK]
    kr = m[:, MLA_Q_RANK + MLA_KV_RANK:]

    lane = lax.broadcasted_iota(jnp.int32, (1, LANES), 1)
    rope_lanes = (lane >= MLA_NOPE_DIM) & (lane < MLA_QK_DIM)

    q2 = _dot(_rms(cq, gqa_ref[...], MLA_Q_RANK).astype(BF16), wq_ref[...])
    q_cos = gq_ref[...] * cosm
    q_sin = gqs_ref[...] * sinm
    scale = MLA_QK_DIM ** -0.5
    for h in range(H):
        qh = q2[:, h * HEAD_PAD:(h + 1) * HEAD_PAD]
        qsw = q2[:, (H + h) * HEAD_PAD:(H + h + 1) * HEAD_PAD]
        ss = jnp.sum(qh * qh, axis=-1, keepdims=True)
        r = lax.rsqrt(ss * (1.0 / MLA_QK_DIM) + EPS) * scale
        qr = (qh * q_cos + qsw * q_sin) * r
        qt_ref[0, h] = qr.T.astype(BF16)

    kv = _dot(_rms(ckv, gkva_ref[...], MLA_KV_RANK).astype(BF16), wkv_ref[...])
    kr_sq = jnp.where(lane < MLA_ROPE_DIM, kr * kr, 0.0)
    ss_rope = jnp.sum(kr_sq, axis=-1, keepdims=True)
    k_rope = jnp.where(rope_lanes, pltpu.roll(kr, MLA_NOPE_DIM, 1), 0.0)
    k_rope_sw = jnp.where(rope_lanes, pltpu.roll(kr, MLA_NOPE_DIM - MLA_ROPE_DIM, 1), 0.0)
    k_cos = gk_ref[...] * cosm
    rope_term = k_rope * k_cos + k_rope_sw * (gks_ref[...] * sinm)
    for h in range(H):
        kn = kv[:, h * HEAD_PAD:(h + 1) * HEAD_PAD]
        ss = jnp.sum(kn * kn, axis=-1, keepdims=True) + ss_rope
        r = lax.rsqrt(ss * (1.0 / MLA_QK_DIM) + EPS)
        k_ref[0, h] = ((kn * k_cos + rope_term) * r).astype(BF16)

    vt = kv[:, H * HEAD_PAD:].T
    row = lax.broadcasted_iota(jnp.int32, (V_ROWS - MLA_V_DIM, tm), 0)
    tail = jnp.where(row == 0, 1.0, 0.0).astype(BF16)
    for h in range(H):
        vt_ref[0, h, :MLA_V_DIM, :] = vt[h * MLA_V_DIM:(h + 1) * MLA_V_DIM].astype(BF16)
        vt_ref[0, h, MLA_V_DIM:, :] = tail


def _mla_prep(mla, tables, gqa, wq, gkva, wkv, gq, gqs, gk, gks, tm):
    B, S, W = mla.shape
    H = MLA_HEADS

    def const(a):
        return pl.BlockSpec(a.shape, lambda b, s: (0,) * a.ndim)

    return pl.pallas_call(
        _mla_prep_kernel,
        out_shape=(jax.ShapeDtypeStruct((B, H, HEAD_PAD, S), BF16),
                   jax.ShapeDtypeStruct((B, H, S, HEAD_PAD), BF16),
                   jax.ShapeDtypeStruct((B, H, V_ROWS, S), BF16)),
        grid=(B, S // tm),
        in_specs=[
            pl.BlockSpec((1, tm, W), lambda b, s: (b, s, 0)),
            pl.BlockSpec((1, tm, LANES), lambda b, s: (b, s, 2)),
            pl.BlockSpec((1, tm, LANES), lambda b, s: (b, s, 3)),
            const(gqa), const(wq), const(gkva), const(wkv),
            const(gq), const(gqs), const(gk), const(gks),
        ],
        out_specs=(pl.BlockSpec((1, H, HEAD_PAD, tm), lambda b, s: (b, 0, 0, s)),
                   pl.BlockSpec((1, H, tm, HEAD_PAD), lambda b, s: (b, 0, s, 0)),
                   pl.BlockSpec((1, H, V_ROWS, tm), lambda b, s: (b, 0, 0, s))),
        compiler_params=pltpu.CompilerParams(
            dimension_semantics=("parallel", "parallel"), vmem_limit_bytes=VMEM_LIMIT),
        name="mla_prep",
    )(mla, tables, tables, gqa, wq, gkva, wkv, gq, gqs, gk, gks)


def _attention_kernel(qt_ref, k_ref, vt_ref, o_ref, *, tk):
    S = k_ref.shape[2]
    tq = qt_ref.shape[3]
    qt = qt_ref[0, 0]
    m = jnp.full((1, tq), -1e30, F32)
    acc = jnp.zeros((V_ROWS, tq), F32)
    for kt in range(S // tk):
        kk = k_ref[0, 0, kt * tk:(kt + 1) * tk, :]
        s = _dot(kk, qt)
        m_new = jnp.maximum(m, jnp.max(s, axis=0, keepdims=True))
        alpha = jnp.exp(m - m_new)
        p = jnp.exp(s - m_new).astype(BF16)
        acc = alpha * acc + _dot(vt_ref[0, 0, :, kt * tk:(kt + 1) * tk], p)
        m = m_new
    o_ref[0] = (acc[:MLA_V_DIM] / acc[MLA_V_DIM:MLA_V_DIM + 1]).astype(o_ref.dtype)


def _attention(qt, k, vt, tq, tk):
    B, H, _, S = qt.shape
    return pl.pallas_call(
        functools.partial(_attention_kernel, tk=tk),
        out_shape=jax.ShapeDtypeStruct((B, H * MLA_V_DIM, S), BF16),
        grid=(B, H, S // tq),
        in_specs=[
            pl.BlockSpec((1, 1, HEAD_PAD, tq), lambda b, h, i: (b, h, 0, i)),
            pl.BlockSpec((1, 1, S, HEAD_PAD), lambda b, h, i: (b, h, 0, 0)),
            pl.BlockSpec((1, 1, V_ROWS, S), lambda b, h, i: (b, h, 0, 0)),
        ],
        out_specs=pl.BlockSpec((1, MLA_V_DIM, tq), lambda b, h, i: (b, h, i)),
        compiler_params=pltpu.CompilerParams(
            dimension_semantics=("parallel", "parallel", "parallel"), vmem_limit_bytes=VMEM_LIMIT),
        name="attention",
    )(qt, k, vt)


def _out_ffn_kernel(x_ref, yr_ref, yt_ref, wo1_ref, wo2_ref, g2_ref, wg_ref, wu_ref, wd_ref, o_ref):
    x = x_ref[0]
    attn = _dot(yr_ref[0], wo1_ref[...]) + _dot_tn(yt_ref[0], wo2_ref[...])
    x1 = x + attn
    h = _rms(x1, g2_ref[...], x1.shape[-1]).astype(BF16)
    gate = _dot(h, wg_ref[...])
    up = _dot(h, wu_ref[...])
    a = (gate * _sigmoid(gate) * up).astype(BF16)
    o_ref[0] = x1 + _dot(a, wd_ref[...])


def _out_ffn(x, y_ret, y_mla_t, wo1, wo2, g2, wg, wu, wd, tm):
    B, S, D = x.shape

    def const(a):
        return pl.BlockSpec(a.shape, lambda b, s: (0,) * a.ndim, pipeline_mode=pl.Buffered(1))

    return pl.pallas_call(
        _out_ffn_kernel,
        out_shape=jax.ShapeDtypeStruct((B, S, D), F32),
        grid=(B, S // tm),
        in_specs=[
            pl.BlockSpec((1, tm, D), lambda b, s: (b, s, 0)),
            pl.BlockSpec((1, tm, RET_WIDTH), lambda b, s: (b, s, 0)),
            pl.BlockSpec((1, MLA_WIDTH, tm), lambda b, s: (b, 0, s)),
            const(wo1), const(wo2), const(g2), const(wg), const(wu), const(wd),
        ],
        out_specs=pl.BlockSpec((1, tm, D), lambda b, s: (b, s, 0)),
        compiler_params=pltpu.CompilerParams(
            dimension_semantics=("parallel", "parallel"), vmem_limit_bytes=VMEM_LIMIT),
        name="out_ffn",
    )(x, y_ret, y_mla_t, wo1, wo2, g2, wg, wu, wd)


def _rotate_half_cols(w):
    half = w.shape[-1] // 2
    return jnp.concatenate([-w[..., half:], w[..., :half]], axis=-1)


def _swap_halves(g):
    half = g.shape[-1] // 2
    return jnp.concatenate([g[..., half:], g[..., :half]], axis=-1)


def _rope_layout():
    hr = RET_HEAD_DIM // 2
    hm = MLA_ROPE_DIM // 2
    fr = ROPE_BASE ** (-jnp.arange(hr, dtype=F32) / hr)
    fm = ROPE_BASE ** (-jnp.arange(hm, dtype=F32) / hm)
    zeros = jnp.zeros((LANES - 2 * hr - 2 * hm,), F32)
    freq = jnp.concatenate([fr, fr, fm, fm, zeros])[None]
    half_pi = jnp.full((1,), math.pi / 2, F32)
    phase = jnp.concatenate([jnp.tile(half_pi, hr), jnp.zeros((hr,), F32),
                             jnp.tile(half_pi, hm), jnp.zeros((hm,), F32), zeros])[None]

    l = jnp.arange(LANES)
    src = jnp.arange(LANES)[:, None]
    ret_cos = (src == (l % hr)[None]).astype(F32)
    ret_sign = jnp.where((l % RET_HEAD_DIM) < hr, -1.0, 1.0)
    ret_sin = (src == (hr + l % hr)[None]).astype(F32) * ret_sign[None]
    on_rope = (l >= MLA_NOPE_DIM) & (l < MLA_QK_DIM)
    j = (l - MLA_NOPE_DIM) % hm
    mla_cos = ((src == (2 * hr + j)[None]) & on_rope[None]).astype(F32)
    mla_sin = ((src == (2 * hr + hm + j)[None]) & on_rope[None]).astype(F32)
    expand = jnp.concatenate([ret_cos, ret_sin, mla_cos, mla_sin], axis=1).astype(BF16)
    bias = jnp.concatenate([jnp.zeros((2 * LANES,), F32), jnp.where(on_rope, 0.0, 1.0),
                            jnp.zeros((LANES,), F32)])[None]
    return freq, phase, expand, bias


def _pad_lanes(a, width):
    return jnp.pad(a, [(0, 0)] * (a.ndim - 1) + [(0, width - a.shape[-1])])


def kernel(x, positions, norm1_g, w_in, ret_decay_logit_fwd, ret_decay_logit_bwd, q_a_norm_g, w_uq,
           kv_a_norm_g, w_ukv, q_norm_g, k_norm_g, w_o, norm2_g, w_gate, w_up, w_down):
    B, S, D = x.shape
    depth = w_in.shape[0]
    H = MLA_HEADS
    tm = min(512, S)
    tq = min(512, S)
    tk = min(256, S)

    freq, phase, expand, bias = _rope_layout()
    tables = _rope_tables(positions.astype(F32)[..., None], freq, phase, expand, bias, tm)

    for layer in range(depth):
        wi = w_in[layer]
        w_rope = wi[:, -MLA_ROPE_DIM:]
        n_in = 4 * RET_WIDTH + 4 * LANES
        wi = _pad_lanes(jnp.concatenate([wi, _rotate_half_cols(w_rope)], axis=1), n_in).astype(BF16)

        wq = w_uq[layer].reshape(MLA_Q_RANK, H, MLA_QK_DIM)
        wq_sw = jnp.concatenate([jnp.zeros_like(wq[..., :MLA_NOPE_DIM]),
                                 _rotate_half_cols(wq[..., MLA_NOPE_DIM:])], axis=-1)
        wq2 = jnp.concatenate([_pad_lanes(wq, HEAD_PAD).reshape(MLA_Q_RANK, H * HEAD_PAD),
                               _pad_lanes(wq_sw, HEAD_PAD).reshape(MLA_Q_RANK, H * HEAD_PAD)],
                              axis=1).astype(BF16)
        wkv = w_ukv[layer].reshape(MLA_KV_RANK, H, MLA_NOPE_DIM + MLA_V_DIM)
        wkv2 = jnp.concatenate(
            [_pad_lanes(wkv[..., :MLA_NOPE_DIM], HEAD_PAD).reshape(MLA_KV_RANK, H * HEAD_PAD),
             wkv[..., MLA_NOPE_DIM:].reshape(MLA_KV_RANK, H * MLA_V_DIM)], axis=1).astype(BF16)

        def head_gain(g):
            g_sw = jnp.concatenate([jnp.zeros((MLA_NOPE_DIM,), F32), _swap_halves(g[MLA_NOPE_DIM:])])
            return _pad_lanes(g, HEAD_PAD)[None], _pad_lanes(g_sw, HEAD_PAD)[None]

        gq, gqs = head_gain(q_norm_g[layer])
        gk, gks = head_gain(k_norm_g[layer])

        def per_pair(logit, reps):
            return jnp.repeat(logit.reshape(RET_HEADS // 2, 1, 2), reps, axis=-1)

        lf, lb = ret_decay_logit_fwd[layer], ret_decay_logit_bwd[layer]

        ret, mla = _in_proj(x, norm1_g[layer][None], wi, tm)
        y_ret = _retention(ret, tables, per_pair(lf, RET_HEAD_DIM), per_pair(lb, RET_HEAD_DIM),
                           per_pair(lf, RET_CHUNK), per_pair(lb, RET_CHUNK))
        qt, k, vt = _mla_prep(mla, tables, q_a_norm_g[layer][None], wq2, kv_a_norm_g[layer][None], wkv2,
                              gq, gqs, gk, gks, tm)
        y_mla_t = _attention(qt, k, vt, tq, tk)
        wo = w_o[layer].astype(BF16)
        x = _out_ffn(x, y_ret, y_mla_t, wo[:RET_WIDTH], wo[RET_WIDTH:], norm2_g[layer][None],
                     w_gate[layer].astype(BF16), w_up[layer].astype(BF16), w_down[layer].astype(BF16), tm)
    return x
```

```python
import functools
import math

import jax
import jax.numpy as jnp
from jax import lax
from jax.experimental import pallas as pl
from jax.experimental.pallas import tpu as pltpu

RET_HEADS = 8
RET_HEAD_DIM = 64
RET_WIDTH = RET_HEADS * RET_HEAD_DIM
RET_CHUNK = 128
MLA_HEADS = 8
MLA_Q_RANK = 256
MLA_KV_RANK = 128
MLA_NOPE_DIM = 64
MLA_ROPE_DIM = 32
MLA_QK_DIM = MLA_NOPE_DIM + MLA_ROPE_DIM
MLA_V_DIM = 64
MLA_WIDTH = MLA_HEADS * MLA_V_DIM
ROPE_BASE = 10000.0
EPS = 1e-6
GN_EPS = 1e-5

LANES = 128
HEAD_PAD = LANES
V_ROWS = 80
RET_UNROLL = 4
GN_ROWS = 1024
VMEM_LIMIT = 56 * 1024 * 1024

F32 = jnp.float32
BF16 = jnp.bfloat16


def _dot(a, b):
    return jnp.dot(a, b, preferred_element_type=F32)


def _dot_nt(a, b):
    return lax.dot_general(a, b, (((1,), (1,)), ((), ())), preferred_element_type=F32)


def _dot_tn(a, b):
    return lax.dot_general(a, b, (((0,), (0,)), ((), ())), preferred_element_type=F32)


def _rms(x, g, n):
    ms = jnp.sum(x * x, axis=-1, keepdims=True) * (1.0 / n)
    return x * lax.rsqrt(ms + EPS) * g


def _sigmoid(x):
    return 1.0 / (1.0 + jnp.exp(-x))


def _hi_lo(x):
    hi = x.astype(BF16)
    lo = (x - hi.astype(F32)).astype(BF16)
    return hi, lo


def _rope_tables_kernel(pos_ref, freq_ref, phase_ref, expand_ref, bias_ref, out_ref):
    ang = pos_ref[0] * freq_ref[...] + phase_ref[...]
    tab = jnp.sin(ang)
    hi, lo = _hi_lo(tab)
    e = expand_ref[...]
    out_ref[0] = _dot(hi, e) + _dot(lo, e) + bias_ref[...]


def _rope_tables(pos, freq, phase, expand, bias, tm):
    B, S, _ = pos.shape
    W = expand.shape[1]
    return pl.pallas_call(
        _rope_tables_kernel,
        out_shape=jax.ShapeDtypeStruct((B, S, W), F32),
        grid=(B, S // tm),
        in_specs=[
            pl.BlockSpec((1, tm, 1), lambda b, s: (b, s, 0)),
            pl.BlockSpec((1, LANES), lambda b, s: (0, 0)),
            pl.BlockSpec((1, LANES), lambda b, s: (0, 0)),
            pl.BlockSpec((LANES, W), lambda b, s: (0, 0)),
            pl.BlockSpec((1, W), lambda b, s: (0, 0)),
        ],
        out_specs=pl.BlockSpec((1, tm, W), lambda b, s: (b, s, 0)),
        compiler_params=pltpu.CompilerParams(
            dimension_semantics=("parallel", "parallel"), vmem_limit_bytes=VMEM_LIMIT),
        name="rope_tables",
    )(pos, freq, phase, expand, bias)


def _in_proj_kernel(x_ref, g_ref, w_ref, ret_ref, mla_ref):
    x = x_ref[0]
    h = _rms(x, g_ref[...], x.shape[-1]).astype(BF16)
    proj = _dot(h, w_ref[...])
    nret = ret_ref.shape[-1]
    ret_ref[0] = proj[:, :nret].astype(BF16)
    mla_ref[0] = proj[:, nret:].astype(BF16)


def _in_proj(x, g, w, tm):
    B, S, D = x.shape
    N = w.shape[1]
    nret = 4 * RET_WIDTH
    return pl.pallas_call(
        _in_proj_kernel,
        out_shape=(jax.ShapeDtypeStruct((B, S, nret), BF16),
                   jax.ShapeDtypeStruct((B, S, N - nret), BF16)),
        grid=(B, S // tm),
        in_specs=[
            pl.BlockSpec((1, tm, D), lambda b, s: (b, s, 0)),
            pl.BlockSpec((1, D), lambda b, s: (0, 0)),
            pl.BlockSpec((D, N), lambda b, s: (0, 0)),
        ],
        out_specs=(pl.BlockSpec((1, tm, nret), lambda b, s: (b, s, 0)),
                   pl.BlockSpec((1, tm, N - nret), lambda b, s: (b, s, 0))),
        compiler_params=pltpu.CompilerParams(
            dimension_semantics=("parallel", "parallel"), vmem_limit_bytes=VMEM_LIMIT),
        name="in_proj",
    )(x, g, w)


def _log_sigmoid(x):
    return jnp.minimum(x, 0.0) - jnp.log(1.0 + jnp.exp(-jnp.abs(x)))


def _retention_kernel(q_ref, k_ref, v_ref, g_ref, cos_ref, sin_ref,
                      lfl_ref, lbl_ref, lfc_ref, lbc_ref, o_ref, qs, ks, kv, st, osc):
    S = q_ref.shape[1]
    C = RET_CHUNK
    n_chunks = S // C
    HD = RET_HEAD_DIM

    lane = lax.broadcasted_iota(jnp.int32, (1, LANES), 1)
    first_half = (lane % HD) < (HD // 2)
    head_a = lane < HD

    def rotate(x):
        swapped = jnp.where(first_half, pltpu.roll(x, LANES - HD // 2, 1), pltpu.roll(x, HD // 2, 1))
        return x * cos_ref[0] + swapped * sin_ref[0]

    qs[...] = rotate(q_ref[0].astype(F32)).astype(BF16)
    ks[...] = (rotate(k_ref[0].astype(F32)) * (HD ** -0.5)).astype(BF16)

    lgf_l = _log_sigmoid(lfl_ref[0])
    lgb_l = _log_sigmoid(lbl_ref[0])
    lgf_c = _log_sigmoid(lfc_ref[0])
    lgb_c = _log_sigmoid(lbc_ref[0])

    row = lax.broadcasted_iota(jnp.int32, (C, 2 * C), 0)
    col = lax.broadcasted_iota(jnp.int32, (C, 2 * C), 1) % C
    diff = (row - col).astype(F32)
    decay = jnp.where(diff >= 0, jnp.exp(lgf_c * jnp.maximum(diff, 0.0)),
                      jnp.exp(lgb_c * jnp.maximum(-diff, 0.0)))

    idx = lax.broadcasted_iota(jnp.int32, (C, LANES), 0).astype(F32)
    zeta_f = jnp.exp(lgf_l * (C - 1 - idx))
    xi_f = jnp.exp(lgf_l * (idx + 1))
    zeta_b = jnp.exp(lgb_l * idx)
    xi_b = jnp.exp(lgb_l * (C - idx))
    gc_f = jnp.exp(lgf_l * C)
    gc_b = jnp.exp(lgb_l * C)

    r2 = lax.broadcasted_iota(jnp.int32, (LANES, LANES), 0)
    c2 = lax.broadcasted_iota(jnp.int32, (LANES, LANES), 1)
    same_head = (r2 < HD) == (c2 < HD)
    avg = jnp.where(same_head, 1.0 / HD, 0.0).astype(BF16)

    def chunk_rows(i):
        return pl.ds(pl.multiple_of(i * C, C), C)

    zero = jnp.zeros((), BF16)
    c3 = lax.broadcasted_iota(jnp.int32, (LANES, 2 * LANES), 1) % LANES
    r3 = lax.broadcasted_iota(jnp.int32, (LANES, 2 * LANES), 0)
    same_head2 = (r3 < HD) == (c3 < HD)

    def kv_body(i, carry):
        sl = chunk_rows(i)
        vc = v_ref[0, sl, :].astype(F32)
        vz = jnp.concatenate([vc * zeta_f, vc * zeta_b], axis=1).astype(BF16)
        kv[i] = jnp.where(same_head2, _dot_tn(ks[sl, :], vz), 0.0)
        return carry

    lax.fori_loop(0, n_chunks, kv_body, 0, unroll=RET_UNROLL)

    def scan_body(j, carry):
        sf, sb = carry
        jb = n_chunks - 1 - j
        st[j, :LANES, :] = sf.astype(BF16)
        st[jb, LANES:, :] = sb.astype(BF16)
        return sf * gc_f + kv[j, :, :LANES], sb * gc_b + kv[jb, :, LANES:]

    zeros = jnp.zeros((LANES, LANES), F32)
    lax.fori_loop(0, n_chunks, scan_body, (zeros, zeros), unroll=2)

    def out_body(i, carry):
        sl = chunk_rows(i)
        qc = qs[sl, :]
        kc = ks[sl, :]
        vc = v_ref[0, sl, :]
        k2 = jnp.concatenate([jnp.where(head_a, kc, zero), jnp.where(head_a, zero, kc)], axis=0)
        v2 = jnp.concatenate([jnp.where(head_a, vc, zero), jnp.where(head_a, zero, vc)], axis=0)
        p = (_dot_nt(qc, k2) * decay).astype(BF16)
        qf = qc.astype(F32)
        qx = jnp.concatenate([qf * xi_f, qf * xi_b], axis=1).astype(BF16)
        osc[sl, :] = _dot(p, v2) + _dot(qx, st[i])
        return carry

    lax.fori_loop(0, n_chunks, out_body, 0, unroll=RET_UNROLL)

    rows = min(S, GN_ROWS)
    for r in range(S // rows):
        rs = slice(r * rows, (r + 1) * rows)
        o = osc[rs, :]
        o_hi, o_lo = _hi_lo(o)
        mu = _dot(o_hi, avg) + _dot(o_lo, avg)
        d = o - mu
        d2_hi, d2_lo = _hi_lo(d * d)
        var = _dot(d2_hi, avg) + _dot(d2_lo, avg)
        gate = g_ref[0, rs, :].astype(F32)
        o_ref[0, rs, :] = (gate * _sigmoid(gate) * d * lax.rsqrt(var + GN_EPS)).astype(o_ref.dtype)


def _retention(ret, tables, lfl, lbl, lfc, lbc):
    B, S, _ = ret.shape
    n_pairs = RET_WIDTH // LANES
    n_chunks = S // RET_CHUNK

    def col(off):
        return pl.BlockSpec((1, S, LANES), lambda b, hp: (b, 0, off + hp))

    def per_pair(width):
        return pl.BlockSpec((1, 1, width), lambda b, hp: (hp, 0, 0))

    return pl.pallas_call(
        _retention_kernel,
        out_shape=jax.ShapeDtypeStruct((B, S, RET_WIDTH), BF16),
        grid=(B, n_pairs),
        in_specs=[
            col(0), col(n_pairs), col(2 * n_pairs), col(3 * n_pairs),
            pl.BlockSpec((1, S, LANES), lambda b, hp: (b, 0, 0)),
            pl.BlockSpec((1, S, LANES), lambda b, hp: (b, 0, 1)),
            per_pair(LANES), per_pair(LANES), per_pair(2 * RET_CHUNK), per_pair(2 * RET_CHUNK),
        ],
        out_specs=pl.BlockSpec((1, S, LANES), lambda b, hp: (b, 0, hp)),
        scratch_shapes=[
            pltpu.VMEM((S, LANES), BF16),
            pltpu.VMEM((S, LANES), BF16),
            pltpu.VMEM((n_chunks, LANES, 2 * LANES), F32),
            pltpu.VMEM((n_chunks, 2 * LANES, LANES), BF16),
            pltpu.VMEM((S, LANES), F32),
        ],
        compiler_params=pltpu.CompilerParams(
            dimension_semantics=("parallel", "parallel"), vmem_limit_bytes=VMEM_LIMIT),
        name="retention",
    )(ret, ret, ret, ret, tables, tables, lfl, lbl, lfc, lbc)


def _mla_prep_kernel(mla_ref, cos_ref, sin_ref, gqa_ref, wq_ref, gkva_ref, wkv_ref,
                     gq_ref, gqs_ref, gk_ref, gks_ref, qt_ref, k_ref, vt_ref):
    H = MLA_HEADS
    tm = mla_ref.shape[1]
    cosm = cos_ref[0]
    sinm = sin_ref[0]
    m = mla_ref[0].astype(F32)
    cq = m[:, :MLA_Q_RANK]
    ckv = m[:, MLA_Q_RANK:MLA_Q_RANK + MLA_KV_RANK]
    kr = m[:, MLA_Q_RANK + MLA_KV_RANK:]

    lane = lax.broadcasted_iota(jnp.int32, (1, LANES), 1)
    rope_lanes = (lane >= MLA_NOPE_DIM) & (lane < MLA_QK_DIM)

    q2 = _dot(_rms(cq, gqa_ref[...], MLA_Q_RANK).astype(BF16), wq_ref[...])
    q_cos = gq_ref[...] * cosm
    q_sin = gqs_ref[...] * sinm
    scale = MLA_QK_DIM ** -0.5 * math.log2(math.e)
    for h in range(H):
        qh = q2[:, h * HEAD_PAD:(h + 1) * HEAD_PAD]
        qsw = q2[:, (H + h) * HEAD_PAD:(H + h + 1) * HEAD_PAD]
        ss = jnp.sum(qh * qh, axis=-1, keepdims=True)
        r = lax.rsqrt(ss * (1.0 / MLA_QK_DIM) + EPS) * scale
        qr = (qh * q_cos + qsw * q_sin) * r
        qt_ref[0, h] = qr.T.astype(BF16)

    kv = _dot(_rms(ckv, gkva_ref[...], MLA_KV_RANK).astype(BF16), wkv_ref[...])
    kr_sq = jnp.where(lane < MLA_ROPE_DIM, kr * kr, 0.0)
    ss_rope = jnp.sum(kr_sq, axis=-1, keepdims=True)
    k_rope = jnp.where(rope_lanes, pltpu.roll(kr, MLA_NOPE_DIM, 1), 0.0)
    k_rope_sw = jnp.where(rope_lanes, pltpu.roll(kr, MLA_NOPE_DIM - MLA_ROPE_DIM, 1), 0.0)
    k_cos = gk_ref[...] * cosm
    rope_term = k_rope * k_cos + k_rope_sw * (gks_ref[...] * sinm)
    for h in range(H):
        kn = kv[:, h * HEAD_PAD:(h + 1) * HEAD_PAD]
        ss = jnp.sum(kn * kn, axis=-1, keepdims=True) + ss_rope
        r = lax.rsqrt(ss * (1.0 / MLA_QK_DIM) + EPS)
        k_ref[0, h] = ((kn * k_cos + rope_term) * r).astype(BF16)

    vt = kv[:, H * HEAD_PAD:].T
    row = lax.broadcasted_iota(jnp.int32, (V_ROWS - MLA_V_DIM, tm), 0)
    tail = jnp.where(row == 0, 1.0, 0.0).astype(BF16)
    for h in range(H):
        vt_ref[0, h, 0, :MLA_V_DIM, :] = vt[h * MLA_V_DIM:(h + 1) * MLA_V_DIM].astype(BF16)
        vt_ref[0, h, 0, MLA_V_DIM:, :] = tail


def _mla_prep(mla, tables, gqa, wq, gkva, wkv, gq, gqs, gk, gks, tm):
    B, S, W = mla.shape
    H = MLA_HEADS

    def const(a):
        return pl.BlockSpec(a.shape, lambda b, s: (0,) * a.ndim)

    return pl.pallas_call(
        _mla_prep_kernel,
        out_shape=(jax.ShapeDtypeStruct((B, H, HEAD_PAD, S), BF16),
                   jax.ShapeDtypeStruct((B, H, S, HEAD_PAD), BF16),
                   jax.ShapeDtypeStruct((B, H, S // tm, V_ROWS, tm), BF16)),
        grid=(B, S // tm),
        in_specs=[
            pl.BlockSpec((1, tm, W), lambda b, s: (b, s, 0)),
            pl.BlockSpec((1, tm, LANES), lambda b, s: (b, s, 2)),
            pl.BlockSpec((1, tm, LANES), lambda b, s: (b, s, 3)),
            const(gqa), const(wq), const(gkva), const(wkv),
            const(gq), const(gqs), const(gk), const(gks),
        ],
        out_specs=(pl.BlockSpec((1, H, HEAD_PAD, tm), lambda b, s: (b, 0, 0, s)),
                   pl.BlockSpec((1, H, tm, HEAD_PAD), lambda b, s: (b, 0, s, 0)),
                   pl.BlockSpec((1, H, 1, V_ROWS, tm), lambda b, s: (b, 0, s, 0, 0))),
        compiler_params=pltpu.CompilerParams(
            dimension_semantics=("parallel", "parallel"), vmem_limit_bytes=VMEM_LIMIT),
        name="mla_prep",
    )(mla, tables, tables, gqa, wq, gkva, wkv, gq, gqs, gk, gks)


def _attention_kernel(qt_ref, k_ref, vt_ref, o_ref, s0, s1, m0, m1, acc_scr):
    i = pl.program_id(0)
    n_k, _, tk = vt_ref.shape[2:]
    tq = qt_ref.shape[3]

    @pl.when(i == 0)
    def _():
        s1[...] = jnp.zeros(s1.shape, F32)
        m1[...] = jnp.zeros(m1.shape, F32)

    def step(s_cur, m_cur, s_prev, m_prev_ref):
        qt = qt_ref[0, 0]
        m_prev = m_prev_ref[...]
        acc_scr[...] = jnp.zeros(acc_scr.shape, F32)
        m8 = jnp.full((8, tq), -jnp.inf, F32)
        for kt in range(n_k):
            rows = slice(kt * tk, (kt + 1) * tk)
            s = _dot(k_ref[0, 0, rows, :], qt)
            s_cur[rows, :] = s
            m8 = jnp.maximum(m8, jnp.max(s.reshape(tk // 8, 8, tq), axis=0))
            p = jnp.exp2(s_prev[rows, :] - m_prev).astype(BF16)
            acc_scr[...] += _dot(vt_ref[0, 0, kt], p)
        m_cur[...] = jnp.max(m8, axis=0, keepdims=True)
        acc = acc_scr[...]
        o_ref[0] = (acc[:MLA_V_DIM] / acc[MLA_V_DIM:MLA_V_DIM + 1]).astype(o_ref.dtype)

    @pl.when(i % 2 == 0)
    def _():
        step(s0, m0, s1, m1)

    @pl.when(i % 2 == 1)
    def _():
        step(s1, m1, s0, m0)


def _attention(qt, k, vt, tq):
    B, H, _, S = qt.shape
    n_k, _, tk = vt.shape[2:]
    n_q = S // tq
    n_tiles = B * H * n_q

    def tile(t):
        return t // (H * n_q), (t // n_q) % H, t % n_q

    def score_tile(i):
        return tile(jnp.minimum(i, n_tiles - 1))

    def value_tile(i):
        return tile(jnp.maximum(i - 1, 0))

    def qt_map(i):
        b, h, q = score_tile(i)
        return b, h, 0, q

    def k_map(i):
        b, h, _ = score_tile(i)
        return b, h, 0, 0

    def vt_map(i):
        b, h, _ = value_tile(i)
        return b, h, 0, 0, 0

    return pl.pallas_call(
        _attention_kernel,
        out_shape=jax.ShapeDtypeStruct((B, H * MLA_V_DIM, S), BF16),
        grid=(n_tiles + 1,),
        in_specs=[
            pl.BlockSpec((1, 1, HEAD_PAD, tq), qt_map),
            pl.BlockSpec((1, 1, S, HEAD_PAD), k_map),
            pl.BlockSpec((1, 1, n_k, V_ROWS, tk), vt_map),
        ],
        out_specs=pl.BlockSpec((1, MLA_V_DIM, tq), value_tile),
        scratch_shapes=[
            pltpu.VMEM((S, tq), F32), pltpu.VMEM((S, tq), F32),
            pltpu.VMEM((1, tq), F32), pltpu.VMEM((1, tq), F32),
            pltpu.VMEM((V_ROWS, tq), F32),
        ],
        compiler_params=pltpu.CompilerParams(
            dimension_semantics=("arbitrary",), vmem_limit_bytes=VMEM_LIMIT),
        name="attention",
    )(qt, k, vt)


def _out_ffn_kernel(x_ref, yr_ref, yt_ref, wo1_ref, wo2_ref, g2_ref, wg_ref, wu_ref, wd_ref, o_ref):
    x = x_ref[0]
    attn = _dot(yr_ref[0], wo1_ref[...]) + _dot_tn(yt_ref[0], wo2_ref[...])
    x1 = x + attn
    h = _rms(x1, g2_ref[...], x1.shape[-1]).astype(BF16)
    gate = _dot(h, wg_ref[...])
    up = _dot(h, wu_ref[...])
    a = (gate * _sigmoid(gate) * up).astype(BF16)
    o_ref[0] = x1 + _dot(a, wd_ref[...])


def _out_ffn(x, y_ret, y_mla_t, wo1, wo2, g2, wg, wu, wd, tm):
    B, S, D = x.shape

    def const(a):
        return pl.BlockSpec(a.shape, lambda b, s: (0,) * a.ndim, pipeline_mode=pl.Buffered(1))

    return pl.pallas_call(
        _out_ffn_kernel,
        out_shape=jax.ShapeDtypeStruct((B, S, D), F32),
        grid=(B, S // tm),
        in_specs=[
            pl.BlockSpec((1, tm, D), lambda b, s: (b, s, 0)),
            pl.BlockSpec((1, tm, RET_WIDTH), lambda b, s: (b, s, 0)),
            pl.BlockSpec((1, MLA_WIDTH, tm), lambda b, s: (b, 0, s)),
            const(wo1), const(wo2), const(g2), const(wg), const(wu), const(wd),
        ],
        out_specs=pl.BlockSpec((1, tm, D), lambda b, s: (b, s, 0)),
        compiler_params=pltpu.CompilerParams(
            dimension_semantics=("parallel", "parallel"), vmem_limit_bytes=VMEM_LIMIT),
        name="out_ffn",
    )(x, y_ret, y_mla_t, wo1, wo2, g2, wg, wu, wd)


def _rotate_half_cols(w):
    half = w.shape[-1] // 2
    return jnp.concatenate([-w[..., half:], w[..., :half]], axis=-1)


def _swap_halves(g):
    half = g.shape[-1] // 2
    return jnp.concatenate([g[..., half:], g[..., :half]], axis=-1)


def _rope_layout():
    hr = RET_HEAD_DIM // 2
    hm = MLA_ROPE_DIM // 2
    fr = ROPE_BASE ** (-jnp.arange(hr, dtype=F32) / hr)
    fm = ROPE_BASE ** (-jnp.arange(hm, dtype=F32) / hm)
    zeros = jnp.zeros((LANES - 2 * hr - 2 * hm,), F32)
    freq = jnp.concatenate([fr, fr, fm, fm, zeros])[None]
    half_pi = jnp.full((1,), math.pi / 2, F32)
    phase = jnp.concatenate([jnp.tile(half_pi, hr), jnp.zeros((hr,), F32),
                             jnp.tile(half_pi, hm), jnp.zeros((hm,), F32), zeros])[None]

    l = jnp.arange(LANES)
    src = jnp.arange(LANES)[:, None]
    ret_cos = (src == (l % hr)[None]).astype(F32)
    ret_sign = jnp.where((l % RET_HEAD_DIM) < hr, -1.0, 1.0)
    ret_sin = (src == (hr + l % hr)[None]).astype(F32) * ret_sign[None]
    on_rope = (l >= MLA_NOPE_DIM) & (l < MLA_QK_DIM)
    j = (l - MLA_NOPE_DIM) % hm
    mla_cos = ((src == (2 * hr + j)[None]) & on_rope[None]).astype(F32)
    mla_sin = ((src == (2 * hr + hm + j)[None]) & on_rope[None]).astype(F32)
    expand = jnp.concatenate([ret_cos, ret_sin, mla_cos, mla_sin], axis=1).astype(BF16)
    bias = jnp.concatenate([jnp.zeros((2 * LANES,), F32), jnp.where(on_rope, 0.0, 1.0),
                            jnp.zeros((LANES,), F32)])[None]
    return freq, phase, expand, bias


def _pad_lanes(a, width):
    return jnp.pad(a, [(0, 0)] * (a.ndim - 1) + [(0, width - a.shape[-1])])


def kernel(x, positions, norm1_g, w_in, ret_decay_logit_fwd, ret_decay_logit_bwd, q_a_norm_g, w_uq,
           kv_a_norm_g, w_ukv, q_norm_g, k_norm_g, w_o, norm2_g, w_gate, w_up, w_down):
    B, S, D = x.shape
    depth = w_in.shape[0]
    H = MLA_HEADS
    tm = min(512, S)
    tq = min(512, S)

    freq, phase, expand, bias = _rope_layout()
    tables = _rope_tables(positions.astype(F32)[..., None], freq, phase, expand, bias, tm)

    for layer in range(depth):
        wi = w_in[layer]
        w_rope = wi[:, -MLA_ROPE_DIM:]
        n_in = 4 * RET_WIDTH + 4 * LANES
        wi = _pad_lanes(jnp.concatenate([wi, _rotate_half_cols(w_rope)], axis=1), n_in).astype(BF16)

        wq = w_uq[layer].reshape(MLA_Q_RANK, H, MLA_QK_DIM)
        wq_sw = jnp.concatenate([jnp.zeros_like(wq[..., :MLA_NOPE_DIM]),
                                 _rotate_half_cols(wq[..., MLA_NOPE_DIM:])], axis=-1)
        wq2 = jnp.concatenate([_pad_lanes(wq, HEAD_PAD).reshape(MLA_Q_RANK, H * HEAD_PAD),
                               _pad_lanes(wq_sw, HEAD_PAD).reshape(MLA_Q_RANK, H * HEAD_PAD)],
                              axis=1).astype(BF16)
        wkv = w_ukv[layer].reshape(MLA_KV_RANK, H, MLA_NOPE_DIM + MLA_V_DIM)
        wkv2 = jnp.concatenate(
            [_pad_lanes(wkv[..., :MLA_NOPE_DIM], HEAD_PAD).reshape(MLA_KV_RANK, H * HEAD_PAD),
             wkv[..., MLA_NOPE_DIM:].reshape(MLA_KV_RANK, H * MLA_V_DIM)], axis=1).astype(BF16)

        def head_gain(g):
            g_sw = jnp.concatenate([jnp.zeros((MLA_NOPE_DIM,), F32), _swap_halves(g[MLA_NOPE_DIM:])])
            return _pad_lanes(g, HEAD_PAD)[None], _pad_lanes(g_sw, HEAD_PAD)[None]

        gq, gqs = head_gain(q_norm_g[layer])
        gk, gks = head_gain(k_norm_g[layer])

        def per_pair(logit, reps):
            return jnp.repeat(logit.reshape(RET_HEADS // 2, 1, 2), reps, axis=-1)

        lf, lb = ret_decay_logit_fwd[layer], ret_decay_logit_bwd[layer]

        ret, mla = _in_proj(x, norm1_g[layer][None], wi, tm)
        y_ret = _retention(ret, tables, per_pair(lf, RET_HEAD_DIM), per_pair(lb, RET_HEAD_DIM),
                           per_pair(lf, RET_CHUNK), per_pair(lb, RET_CHUNK))
        qt, k, vt = _mla_prep(mla, tables, q_a_norm_g[layer][None], wq2, kv_a_norm_g[layer][None], wkv2,
                              gq, gqs, gk, gks, tm)
        y_mla_t = _attention(qt, k, vt, tq)
        wo = w_o[layer].astype(BF16)
        x = _out_ffn(x, y_ret, y_mla_t, wo[:RET_WIDTH], wo[RET_WIDTH:], norm2_g[layer][None],
                     w_gate[layer].astype(BF16), w_up[layer].astype(BF16), w_down[layer].astype(BF16), tm)
    return x
```

```python
import functools
import math

import jax
import jax.numpy as jnp
from jax import lax
from jax.experimental import pallas as pl
from jax.experimental.pallas import tpu as pltpu

RET_HEADS = 8
RET_HEAD_DIM = 64
RET_WIDTH = RET_HEADS * RET_HEAD_DIM
RET_CHUNK = 128
MLA_HEADS = 8
MLA_Q_RANK = 256
MLA_KV_RANK = 128
MLA_NOPE_DIM = 64
MLA_ROPE_DIM = 32
MLA_QK_DIM = MLA_NOPE_DIM + MLA_ROPE_DIM
MLA_V_DIM = 64
MLA_WIDTH = MLA_HEADS * MLA_V_DIM
ROPE_BASE = 10000.0
EPS = 1e-6
GN_EPS = 1e-5

LANES = 128
HEAD_PAD = LANES
V_ROWS = 80
RET_UNROLL = 4
GN_ROWS = 1024
ATT_SUB_TILE = 512
VMEM_LIMIT = 56 * 1024 * 1024

F32 = jnp.float32
BF16 = jnp.bfloat16


def _dot(a, b):
    return jnp.dot(a, b, preferred_element_type=F32)


def _dot_nt(a, b):
    return lax.dot_general(a, b, (((1,), (1,)), ((), ())), preferred_element_type=F32)


def _dot_tn(a, b):
    return lax.dot_general(a, b, (((0,), (0,)), ((), ())), preferred_element_type=F32)


def _rms(x, g, n):
    ms = jnp.sum(x * x, axis=-1, keepdims=True) * (1.0 / n)
    return x * lax.rsqrt(ms + EPS) * g


def _sigmoid(x):
    return 1.0 / (1.0 + jnp.exp(-x))


def _hi_lo(x):
    hi = x.astype(BF16)
    lo = (x - hi.astype(F32)).astype(BF16)
    return hi, lo


def _rope_tables_kernel(pos_ref, freq_ref, phase_ref, expand_ref, bias_ref, out_ref, out_t_ref):
    ang = pos_ref[0] * freq_ref[...] + phase_ref[...]
    tab = jnp.sin(ang)
    hi, lo = _hi_lo(tab)
    e = expand_ref[...]
    full = _dot(hi, e) + _dot(lo, e) + bias_ref[...]
    out_ref[0] = full
    out_t_ref[0] = full[:, 2 * LANES:].T


def _rope_tables(pos, freq, phase, expand, bias, tm):
    B, S, _ = pos.shape
    W = expand.shape[1]
    return pl.pallas_call(
        _rope_tables_kernel,
        out_shape=(jax.ShapeDtypeStruct((B, S, W), F32),
                   jax.ShapeDtypeStruct((B, W - 2 * LANES, S), F32)),
        grid=(B, S // tm),
        in_specs=[
            pl.BlockSpec((1, tm, 1), lambda b, s: (b, s, 0)),
            pl.BlockSpec((1, LANES), lambda b, s: (0, 0)),
            pl.BlockSpec((1, LANES), lambda b, s: (0, 0)),
            pl.BlockSpec((LANES, W), lambda b, s: (0, 0)),
            pl.BlockSpec((1, W), lambda b, s: (0, 0)),
        ],
        out_specs=(pl.BlockSpec((1, tm, W), lambda b, s: (b, s, 0)),
                   pl.BlockSpec((1, W - 2 * LANES, tm), lambda b, s: (b, 0, s))),
        compiler_params=pltpu.CompilerParams(
            dimension_semantics=("parallel", "parallel"), vmem_limit_bytes=VMEM_LIMIT),
        name="rope_tables",
    )(pos, freq, phase, expand, bias)


def _in_proj_kernel(x_ref, g_ref, w_ref, ret_ref, mla_ref):
    x = x_ref[0]
    h = _rms(x, g_ref[...], x.shape[-1]).astype(BF16)
    proj = _dot(h, w_ref[...])
    nret = ret_ref.shape[-1]
    ret_ref[0] = proj[:, :nret].astype(BF16)
    mla_ref[0] = proj[:, nret:].astype(BF16)


def _in_proj(x, g, w, tm):
    B, S, D = x.shape
    N = w.shape[1]
    nret = 4 * RET_WIDTH
    return pl.pallas_call(
        _in_proj_kernel,
        out_shape=(jax.ShapeDtypeStruct((B, S, nret), BF16),
                   jax.ShapeDtypeStruct((B, S, N - nret), BF16)),
        grid=(B, S // tm),
        in_specs=[
            pl.BlockSpec((1, tm, D), lambda b, s: (b, s, 0)),
            pl.BlockSpec((1, D), lambda b, s: (0, 0)),
            pl.BlockSpec((D, N), lambda b, s: (0, 0)),
        ],
        out_specs=(pl.BlockSpec((1, tm, nret), lambda b, s: (b, s, 0)),
                   pl.BlockSpec((1, tm, N - nret), lambda b, s: (b, s, 0))),
        compiler_params=pltpu.CompilerParams(
            dimension_semantics=("parallel", "parallel"), vmem_limit_bytes=VMEM_LIMIT),
        name="in_proj",
    )(x, g, w)


def _log_sigmoid(x):
    return jnp.minimum(x, 0.0) - jnp.log(1.0 + jnp.exp(-jnp.abs(x)))


def _retention_kernel(q_ref, k_ref, v_ref, g_ref, cos_ref, sin_ref,
                      lfl_ref, lbl_ref, lfc_ref, lbc_ref, o_ref, qs, ks, kv, st, osc):
    S = q_ref.shape[1]
    C = RET_CHUNK
    n_chunks = S // C
    HD = RET_HEAD_DIM

    lane = lax.broadcasted_iota(jnp.int32, (1, LANES), 1)
    head_a = lane < HD

    r2 = lax.broadcasted_iota(jnp.int32, (LANES, LANES), 0)
    c2 = lax.broadcasted_iota(jnp.int32, (LANES, LANES), 1)
    partner = jnp.where((c2 % HD) < (HD // 2), c2 + HD // 2, c2 - HD // 2)
    swap = (r2 == partner).astype(BF16)

    def rotate(x):
        return x.astype(F32) * cos_ref[0] + _dot(x, swap) * sin_ref[0]

    qs[...] = rotate(q_ref[0]).astype(BF16)
    ks[...] = rotate(k_ref[0]).astype(BF16)
    k_scale = HD ** -0.5

    lgf_l = _log_sigmoid(lfl_ref[0])
    lgb_l = _log_sigmoid(lbl_ref[0])
    lgf_c = _log_sigmoid(lfc_ref[0])
    lgb_c = _log_sigmoid(lbc_ref[0])

    row = lax.broadcasted_iota(jnp.int32, (C, 2 * C), 0)
    col = lax.broadcasted_iota(jnp.int32, (C, 2 * C), 1) % C
    diff = (row - col).astype(F32)
    decay = k_scale * jnp.where(diff >= 0, jnp.exp(lgf_c * jnp.maximum(diff, 0.0)),
                                jnp.exp(lgb_c * jnp.maximum(-diff, 0.0)))

    idx = lax.broadcasted_iota(jnp.int32, (C, LANES), 0).astype(F32)
    zeta_f = k_scale * jnp.exp(lgf_l * (C - 1 - idx))
    xi_f = jnp.exp(lgf_l * (idx + 1))
    zeta_b = k_scale * jnp.exp(lgb_l * idx)
    xi_b = jnp.exp(lgb_l * (C - idx))
    gc_f = jnp.exp(lgf_l * C)
    gc_b = jnp.exp(lgb_l * C)

    same_head = (r2 < HD) == (c2 < HD)
    avg = jnp.where(same_head, 1.0 / HD, 0.0).astype(BF16)

    def chunk_rows(i):
        return pl.ds(pl.multiple_of(i * C, C), C)

    zero = jnp.zeros((), BF16)
    c3 = lax.broadcasted_iota(jnp.int32, (LANES, 2 * LANES), 1) % LANES
    r3 = lax.broadcasted_iota(jnp.int32, (LANES, 2 * LANES), 0)
    same_head2 = (r3 < HD) == (c3 < HD)

    def kv_body(i, carry):
        sl = chunk_rows(i)
        vc = v_ref[0, sl, :].astype(F32)
        vz = jnp.concatenate([vc * zeta_f, vc * zeta_b], axis=1).astype(BF16)
        kv[i] = jnp.where(same_head2, _dot_tn(ks[sl, :], vz), 0.0)
        return carry

    lax.fori_loop(0, n_chunks, kv_body, 0, unroll=RET_UNROLL)

    def scan_body(j, carry):
        sf, sb = carry
        jb = n_chunks - 1 - j
        st[j, :LANES, :] = sf.astype(BF16)
        st[jb, LANES:, :] = sb.astype(BF16)
        return sf * gc_f + kv[j, :, :LANES], sb * gc_b + kv[jb, :, LANES:]

    zeros = jnp.zeros((LANES, LANES), F32)
    lax.fori_loop(0, n_chunks, scan_body, (zeros, zeros), unroll=2)

    def out_body(i, carry):
        sl = chunk_rows(i)
        qc = qs[sl, :]
        kc = ks[sl, :]
        vc = v_ref[0, sl, :]
        k2 = jnp.concatenate([jnp.where(head_a, kc, zero), jnp.where(head_a, zero, kc)], axis=0)
        v2 = jnp.concatenate([jnp.where(head_a, vc, zero), jnp.where(head_a, zero, vc)], axis=0)
        p = (_dot_nt(qc, k2) * decay).astype(BF16)
        qf = qc.astype(F32)
        qx = jnp.concatenate([qf * xi_f, qf * xi_b], axis=1).astype(BF16)
        osc[sl, :] = _dot(p, v2) + _dot(qx, st[i])
        return carry

    lax.fori_loop(0, n_chunks, out_body, 0, unroll=RET_UNROLL)

    rows = min(S, GN_ROWS)
    for r in range(S // rows):
        rs = slice(r * rows, (r + 1) * rows)
        o = osc[rs, :]
        o_hi, o_lo = _hi_lo(o)
        mu = _dot(o_hi, avg) + _dot(o_lo, avg)
        d = o - mu
        var = _dot((d * d).astype(BF16), avg)
        gate = g_ref[0, rs, :].astype(F32)
        o_ref[0, rs, :] = (gate * _sigmoid(gate) * d * lax.rsqrt(var + GN_EPS)).astype(o_ref.dtype)


def _retention(ret, tables, lfl, lbl, lfc, lbc):
    B, S, _ = ret.shape
    n_pairs = RET_WIDTH // LANES
    n_chunks = S // RET_CHUNK

    def col(off):
        return pl.BlockSpec((1, S, LANES), lambda b, hp: (b, 0, off + hp))

    def per_pair(width):
        return pl.BlockSpec((1, 1, width), lambda b, hp: (hp, 0, 0))

    return pl.pallas_call(
        _retention_kernel,
        out_shape=jax.ShapeDtypeStruct((B, S, RET_WIDTH), BF16),
        grid=(B, n_pairs),
        in_specs=[
            col(0), col(n_pairs), col(2 * n_pairs), col(3 * n_pairs),
            pl.BlockSpec((1, S, LANES), lambda b, hp: (b, 0, 0)),
            pl.BlockSpec((1, S, LANES), lambda b, hp: (b, 0, 1)),
            per_pair(LANES), per_pair(LANES), per_pair(2 * RET_CHUNK), per_pair(2 * RET_CHUNK),
        ],
        out_specs=pl.BlockSpec((1, S, LANES), lambda b, hp: (b, 0, hp)),
        scratch_shapes=[
            pltpu.VMEM((S, LANES), BF16),
            pltpu.VMEM((S, LANES), BF16),
            pltpu.VMEM((n_chunks, LANES, 2 * LANES), F32),
            pltpu.VMEM((n_chunks, 2 * LANES, LANES), BF16),
            pltpu.VMEM((S, LANES), F32),
        ],
        compiler_params=pltpu.CompilerParams(
            dimension_semantics=("parallel", "parallel"), vmem_limit_bytes=VMEM_LIMIT),
        name="retention",
    )(ret, ret, ret, ret, tables, tables, lfl, lbl, lfc, lbc)


def _mla_prep_kernel(mla_ref, cos_ref, sin_ref, tabt_ref, gqa_ref, wqt_ref, gkva_ref, wk_ref, wvt_ref,
                     gqc_ref, gqsc_ref, gk_ref, gks_ref, qt_ref, k_ref, vt_ref):
    H = MLA_HEADS
    tm = mla_ref.shape[1]
    m = mla_ref[0].astype(F32)
    cq = m[:, :MLA_Q_RANK]
    ckv = m[:, MLA_Q_RANK:MLA_Q_RANK + MLA_KV_RANK]
    kr = m[:, MLA_Q_RANK + MLA_KV_RANK:]

    cq_t = _rms(cq, gqa_ref[...], MLA_Q_RANK).T.astype(BF16)
    q2t = _dot(wqt_ref[...], cq_t)
    q_cos = gqc_ref[...] * tabt_ref[0, :HEAD_PAD, :]
    q_sin = gqsc_ref[...] * tabt_ref[0, HEAD_PAD:, :]
    scale = MLA_QK_DIM ** -0.5 * math.log2(math.e)
    for h in range(H):
        qh = q2t[h * HEAD_PAD:(h + 1) * HEAD_PAD]
        qsw = q2t[(H + h) * HEAD_PAD:(H + h + 1) * HEAD_PAD]
        ss = jnp.sum(qh * qh, axis=0, keepdims=True)
        r = lax.rsqrt(ss * (1.0 / MLA_QK_DIM) + EPS) * scale
        qt_ref[0, h] = ((qh * q_cos + qsw * q_sin) * r).astype(BF16)

    ckv_n = _rms(ckv, gkva_ref[...], MLA_KV_RANK)
    kn_all = _dot(ckv_n.astype(BF16), wk_ref[...])
    cosm = cos_ref[0]
    sinm = sin_ref[0]
    lane = lax.broadcasted_iota(jnp.int32, (1, LANES), 1)
    rope_lanes = (lane >= MLA_NOPE_DIM) & (lane < MLA_QK_DIM)
    kr_sq = jnp.where(lane < MLA_ROPE_DIM, kr * kr, 0.0)
    ss_rope = jnp.sum(kr_sq, axis=-1, keepdims=True)
    k_rope = jnp.where(rope_lanes, pltpu.roll(kr, MLA_NOPE_DIM, 1), 0.0)
    k_rope_sw = jnp.where(rope_lanes, pltpu.roll(kr, MLA_NOPE_DIM - MLA_ROPE_DIM, 1), 0.0)
    k_cos = gk_ref[...] * cosm
    rope_term = k_rope * k_cos + k_rope_sw * (gks_ref[...] * sinm)
    for h in range(H):
        kn = kn_all[:, h * HEAD_PAD:(h + 1) * HEAD_PAD]
        ss = jnp.sum(kn * kn, axis=-1, keepdims=True) + ss_rope
        r = lax.rsqrt(ss * (1.0 / MLA_QK_DIM) + EPS)
        k_ref[0, h] = ((kn * k_cos + rope_term) * r).astype(BF16)

    vt = _dot(wvt_ref[...], ckv_n.T.astype(BF16))
    row = lax.broadcasted_iota(jnp.int32, (V_ROWS - MLA_V_DIM, tm), 0)
    tail = jnp.where(row == 0, 1.0, 0.0).astype(BF16)
    for h in range(H):
        vt_ref[0, h, 0, :MLA_V_DIM, :] = vt[h * MLA_V_DIM:(h + 1) * MLA_V_DIM].astype(BF16)
        vt_ref[0, h, 0, MLA_V_DIM:, :] = tail


def _mla_prep(mla, tables, tables_t, gqa, wqt, gkva, wk, wvt, gqc, gqsc, gk, gks, tm):
    B, S, W = mla.shape
    H = MLA_HEADS

    def const(a):
        return pl.BlockSpec(a.shape, lambda b, s: (0,) * a.ndim)

    return pl.pallas_call(
        _mla_prep_kernel,
        out_shape=(jax.ShapeDtypeStruct((B, H, HEAD_PAD, S), BF16),
                   jax.ShapeDtypeStruct((B, H, S, HEAD_PAD), BF16),
                   jax.ShapeDtypeStruct((B, H, S // tm, V_ROWS, tm), BF16)),
        grid=(B, S // tm),
        in_specs=[
            pl.BlockSpec((1, tm, W), lambda b, s: (b, s, 0)),
            pl.BlockSpec((1, tm, LANES), lambda b, s: (b, s, 2)),
            pl.BlockSpec((1, tm, LANES), lambda b, s: (b, s, 3)),
            pl.BlockSpec((1, 2 * HEAD_PAD, tm), lambda b, s: (b, 0, s)),
            const(gqa), const(wqt), const(gkva), const(wk), const(wvt),
            const(gqc), const(gqsc), const(gk), const(gks),
        ],
        out_specs=(pl.BlockSpec((1, H, HEAD_PAD, tm), lambda b, s: (b, 0, 0, s)),
                   pl.BlockSpec((1, H, tm, HEAD_PAD), lambda b, s: (b, 0, s, 0)),
                   pl.BlockSpec((1, H, 1, V_ROWS, tm), lambda b, s: (b, 0, s, 0, 0))),
        compiler_params=pltpu.CompilerParams(
            dimension_semantics=("parallel", "parallel"), vmem_limit_bytes=VMEM_LIMIT),
        name="mla_prep",
    )(mla, tables, tables, tables_t, gqa, wqt, gkva, wk, wvt, gqc, gqsc, gk, gks)


def _attention_kernel(qt_ref, k_ref, vt_ref, o_ref, s0, s1, m0, m1, acc_scr):
    i = pl.program_id(0)
    n_k, _, tk = vt_ref.shape[2:]
    n_sub, _, tqs = s0.shape

    @pl.when(i == 0)
    def _():
        s1[...] = jnp.zeros(s1.shape, F32)
        m1[...] = jnp.zeros(m1.shape, F32)

    def step(s_cur, m_cur, s_prev, m_prev_ref):
        for sub in range(n_sub):
            cols = slice(sub * tqs, (sub + 1) * tqs)
            qt = qt_ref[0, 0, :, cols]
            m_prev = m_prev_ref[sub]
            acc_scr[sub] = jnp.zeros(acc_scr.shape[1:], F32)
            m8 = jnp.full((8, tqs), -jnp.inf, F32)
            for kt in range(n_k):
                rows = slice(kt * tk, (kt + 1) * tk)
                s = _dot(k_ref[0, 0, rows, :], qt)
                s_cur[sub, rows, :] = s
                m8 = jnp.maximum(m8, jnp.max(s.reshape(tk // 8, 8, tqs), axis=0))
                p = jnp.exp2(s_prev[sub, rows, :] - m_prev).astype(BF16)
                acc_scr[sub] += _dot(vt_ref[0, 0, kt], p)
            m_cur[sub] = jnp.max(m8, axis=0, keepdims=True)
            acc = acc_scr[sub]
            o_ref[0, :, cols] = (acc[:MLA_V_DIM] / acc[MLA_V_DIM:MLA_V_DIM + 1]).astype(o_ref.dtype)

    @pl.when(i % 2 == 0)
    def _():
        step(s0, m0, s1, m1)

    @pl.when(i % 2 == 1)
    def _():
        step(s1, m1, s0, m0)


def _attention(qt, k, vt, tq):
    B, H, _, S = qt.shape
    n_k, _, tk = vt.shape[2:]
    n_q = S // tq
    n_tiles = B * H * n_q
    tqs = min(tq, ATT_SUB_TILE)
    n_sub = tq // tqs

    def tile(t):
        return t // (H * n_q), (t // n_q) % H, t % n_q

    def score_tile(i):
        return tile(jnp.minimum(i, n_tiles - 1))

    def value_tile(i):
        return tile(jnp.maximum(i - 1, 0))

    def qt_map(i):
        b, h, q = score_tile(i)
        return b, h, 0, q

    def k_map(i):
        b, h, _ = score_tile(i)
        return b, h, 0, 0

    def vt_map(i):
        b, h, _ = value_tile(i)
        return b, h, 0, 0, 0

    return pl.pallas_call(
        _attention_kernel,
        out_shape=jax.ShapeDtypeStruct((B, H * MLA_V_DIM, S), BF16),
        grid=(n_tiles + 1,),
        in_specs=[
            pl.BlockSpec((1, 1, HEAD_PAD, tq), qt_map),
            pl.BlockSpec((1, 1, S, HEAD_PAD), k_map),
            pl.BlockSpec((1, 1, n_k, V_ROWS, tk), vt_map),
        ],
        out_specs=pl.BlockSpec((1, MLA_V_DIM, tq), value_tile),
        scratch_shapes=[
            pltpu.VMEM((n_sub, S, tqs), F32), pltpu.VMEM((n_sub, S, tqs), F32),
            pltpu.VMEM((n_sub, 1, tqs), F32), pltpu.VMEM((n_sub, 1, tqs), F32),
            pltpu.VMEM((n_sub, V_ROWS, tqs), F32),
        ],
        compiler_params=pltpu.CompilerParams(
            dimension_semantics=("arbitrary",), vmem_limit_bytes=VMEM_LIMIT),
        name="attention",
    )(qt, k, vt)


def _out_ffn_kernel(x_ref, yr_ref, yt_ref, wo1_ref, wo2_ref, g2_ref, wg_ref, wu_ref, wd_ref, o_ref):
    x = x_ref[0]
    attn = _dot(yr_ref[0], wo1_ref[...]) + _dot_tn(yt_ref[0], wo2_ref[...])
    x1 = x + attn
    h = _rms(x1, g2_ref[...], x1.shape[-1]).astype(BF16)
    gate = _dot(h, wg_ref[...])
    up = _dot(h, wu_ref[...])
    a = (gate * _sigmoid(gate) * up).astype(BF16)
    o_ref[0] = x1 + _dot(a, wd_ref[...])


def _out_ffn(x, y_ret, y_mla_t, wo1, wo2, g2, wg, wu, wd, tm):
    B, S, D = x.shape

    def const(a):
        return pl.BlockSpec(a.shape, lambda b, s: (0,) * a.ndim, pipeline_mode=pl.Buffered(1))

    return pl.pallas_call(
        _out_ffn_kernel,
        out_shape=jax.ShapeDtypeStruct((B, S, D), F32),
        grid=(B, S // tm),
        in_specs=[
            pl.BlockSpec((1, tm, D), lambda b, s: (b, s, 0)),
            pl.BlockSpec((1, tm, RET_WIDTH), lambda b, s: (b, s, 0)),
            pl.BlockSpec((1, MLA_WIDTH, tm), lambda b, s: (b, 0, s)),
            const(wo1), const(wo2), const(g2), const(wg), const(wu), const(wd),
        ],
        out_specs=pl.BlockSpec((1, tm, D), lambda b, s: (b, s, 0)),
        compiler_params=pltpu.CompilerParams(
            dimension_semantics=("parallel", "parallel"), vmem_limit_bytes=VMEM_LIMIT),
        name="out_ffn",
    )(x, y_ret, y_mla_t, wo1, wo2, g2, wg, wu, wd)


def _rotate_half_cols(w):
    half = w.shape[-1] // 2
    return jnp.concatenate([-w[..., half:], w[..., :half]], axis=-1)


def _swap_halves(g):
    half = g.shape[-1] // 2
    return jnp.concatenate([g[..., half:], g[..., :half]], axis=-1)


def _rope_layout():
    hr = RET_HEAD_DIM // 2
    hm = MLA_ROPE_DIM // 2
    fr = ROPE_BASE ** (-jnp.arange(hr, dtype=F32) / hr)
    fm = ROPE_BASE ** (-jnp.arange(hm, dtype=F32) / hm)
    zeros = jnp.zeros((LANES - 2 * hr - 2 * hm,), F32)
    freq = jnp.concatenate([fr, fr, fm, fm, zeros])[None]
    half_pi = jnp.full((1,), math.pi / 2, F32)
    phase = jnp.concatenate([jnp.tile(half_pi, hr), jnp.zeros((hr,), F32),
                             jnp.tile(half_pi, hm), jnp.zeros((hm,), F32), zeros])[None]

    l = jnp.arange(LANES)
    src = jnp.arange(LANES)[:, None]
    ret_cos = (src == (l % hr)[None]).astype(F32)
    ret_sign = jnp.where((l % RET_HEAD_DIM) < hr, -1.0, 1.0)
    ret_sin = (src == (hr + l % hr)[None]).astype(F32) * ret_sign[None]
    on_rope = (l >= MLA_NOPE_DIM) & (l < MLA_QK_DIM)
    j = (l - MLA_NOPE_DIM) % hm
    mla_cos = ((src == (2 * hr + j)[None]) & on_rope[None]).astype(F32)
    mla_sin = ((src == (2 * hr + hm + j)[None]) & on_rope[None]).astype(F32)
    expand = jnp.concatenate([ret_cos, ret_sin, mla_cos, mla_sin], axis=1).astype(BF16)
    bias = jnp.concatenate([jnp.zeros((2 * LANES,), F32), jnp.where(on_rope, 0.0, 1.0),
                            jnp.zeros((LANES,), F32)])[None]
    return freq, phase, expand, bias


def _pad_lanes(a, width):
    return jnp.pad(a, [(0, 0)] * (a.ndim - 1) + [(0, width - a.shape[-1])])


def kernel(x, positions, norm1_g, w_in, ret_decay_logit_fwd, ret_decay_logit_bwd, q_a_norm_g, w_uq,
           kv_a_norm_g, w_ukv, q_norm_g, k_norm_g, w_o, norm2_g, w_gate, w_up, w_down):
    B, S, D = x.shape
    depth = w_in.shape[0]
    H = MLA_HEADS
    tm = min(512, S)
    tq = min(1024, S)

    freq, phase, expand, bias = _rope_layout()
    tables, tables_t = _rope_tables(positions.astype(F32)[..., None], freq, phase, expand, bias, tm)

    for layer in range(depth):
        wi = w_in[layer]
        w_rope = wi[:, -MLA_ROPE_DIM:]
        n_in = 4 * RET_WIDTH + 4 * LANES
        wi = _pad_lanes(jnp.concatenate([wi, _rotate_half_cols(w_rope)], axis=1), n_in).astype(BF16)

        wq = w_uq[layer].reshape(MLA_Q_RANK, H, MLA_QK_DIM)
        wq_sw = jnp.concatenate([jnp.zeros_like(wq[..., :MLA_NOPE_DIM]),
                                 _rotate_half_cols(wq[..., MLA_NOPE_DIM:])], axis=-1)
        wqt = jnp.concatenate([_pad_lanes(wq, HEAD_PAD).reshape(MLA_Q_RANK, H * HEAD_PAD),
                               _pad_lanes(wq_sw, HEAD_PAD).reshape(MLA_Q_RANK, H * HEAD_PAD)],
                              axis=1).T.astype(BF16)
        wkv = w_ukv[layer].reshape(MLA_KV_RANK, H, MLA_NOPE_DIM + MLA_V_DIM)
        wk = _pad_lanes(wkv[..., :MLA_NOPE_DIM], HEAD_PAD).reshape(MLA_KV_RANK, H * HEAD_PAD).astype(BF16)
        wvt = wkv[..., MLA_NOPE_DIM:].reshape(MLA_KV_RANK, H * MLA_V_DIM).T.astype(BF16)

        def head_gain(g):
            g_sw = jnp.concatenate([jnp.zeros((MLA_NOPE_DIM,), F32), _swap_halves(g[MLA_NOPE_DIM:])])
            return _pad_lanes(g, HEAD_PAD)[None], _pad_lanes(g_sw, HEAD_PAD)[None]

        gq, gqs = head_gain(q_norm_g[layer])
        gk, gks = head_gain(k_norm_g[layer])

        def per_pair(logit, reps):
            return jnp.repeat(logit.reshape(RET_HEADS // 2, 1, 2), reps, axis=-1)

        lf, lb = ret_decay_logit_fwd[layer], ret_decay_logit_bwd[layer]

        ret, mla = _in_proj(x, norm1_g[layer][None], wi, tm)
        y_ret = _retention(ret, tables, per_pair(lf, RET_HEAD_DIM), per_pair(lb, RET_HEAD_DIM),
                           per_pair(lf, RET_CHUNK), per_pair(lb, RET_CHUNK))
        qt, k, vt = _mla_prep(mla, tables, tables_t, q_a_norm_g[layer][None], wqt, kv_a_norm_g[layer][None],
                              wk, wvt, gq.T, gqs.T, gk, gks, tm)
        y_mla_t = _attention(qt, k, vt, tq)
        wo = w_o[layer].astype(BF16)
        x = _out_ffn(x, y_ret, y_mla_t, wo[:RET_WIDTH], wo[RET_WIDTH:], norm2_g[layer][None],
                     w_gate[layer].astype(BF16), w_up[layer].astype(BF16), w_down[layer].astype(BF16), tm)
    return x
```

```python
import math

import jax
import jax.numpy as jnp
from jax import lax
from jax.experimental import pallas as pl
from jax.experimental.pallas import tpu as pltpu

RET_HEADS = 8
RET_HEAD_DIM = 64
RET_WIDTH = RET_HEADS * RET_HEAD_DIM
RET_CHUNK = 128
MLA_HEADS = 8
MLA_Q_RANK = 256
MLA_KV_RANK = 128
MLA_NOPE_DIM = 64
MLA_ROPE_DIM = 32
MLA_QK_DIM = MLA_NOPE_DIM + MLA_ROPE_DIM
MLA_V_DIM = 64
MLA_WIDTH = MLA_HEADS * MLA_V_DIM
ROPE_BASE = 10000.0
EPS = 1e-6
GN_EPS = 1e-5

LANES = 128
HEAD_PAD = LANES
RET_UNROLL = 4
GN_ROWS = 1024
ATT_SUB_TILE = 512
FRONT_ROWS = 256
VMEM_LIMIT = 56 * 1024 * 1024

F32 = jnp.float32
BF16 = jnp.bfloat16


def _dot(a, b):
    return jnp.dot(a, b, preferred_element_type=F32)


def _dot_nt(a, b):
    return lax.dot_general(a, b, (((1,), (1,)), ((), ())), preferred_element_type=F32)


def _dot_tn(a, b):
    return lax.dot_general(a, b, (((0,), (0,)), ((), ())), preferred_element_type=F32)


def _rms(x, g, n):
    ms = jnp.sum(x * x, axis=-1, keepdims=True) * (1.0 / n)
    return x * lax.rsqrt(ms + EPS) * g


def _sigmoid(x):
    return 1.0 / (1.0 + jnp.exp(-x))


def _hi_lo(x):
    hi = x.astype(BF16)
    lo = (x - hi.astype(F32)).astype(BF16)
    return hi, lo


def _front_kernel(pos_ref, freq_ref, phase_ref, expand_ref, bias_ref, x_ref, g1_ref, win_ref,
                  gqa_ref, wqt_ref, gkva_ref, wk_ref, wvt_ref, gqc_ref, gqsc_ref, gk_ref, gks_ref,
                  ret_ref, tab_ref, qt_ref, k_ref, vt_ref):
    H = MLA_HEADS
    tm = x_ref.shape[1]
    nret = ret_ref.shape[-1]
    lane = lax.broadcasted_iota(jnp.int32, (1, LANES), 1)
    rope_lanes = (lane >= MLA_NOPE_DIM) & (lane < MLA_QK_DIM)
    scale = MLA_QK_DIM ** -0.5 * math.log2(math.e)

    def tables(rows):
        tab = jnp.sin(pos_ref[0, rows, :] * freq_ref[...] + phase_ref[...])
        hi, lo = _hi_lo(tab)
        e = expand_ref[...]
        full = _dot(hi, e) + _dot(lo, e) + bias_ref[...]
        tab_ref[0, rows, :] = full[:, :2 * LANES]
        return full[:, 2 * LANES:]

    def project(rows):
        x = x_ref[0, rows, :]
        h = _rms(x, g1_ref[...], x.shape[-1]).astype(BF16)
        proj = _dot(h, win_ref[...])
        ret_ref[0, rows, :] = proj[:, :nret].astype(BF16)
        return proj[:, nret:]

    def mla_prep(rows, mtab, m):
        cq = m[:, :MLA_Q_RANK]
        ckv = m[:, MLA_Q_RANK:MLA_Q_RANK + MLA_KV_RANK]
        kr = m[:, MLA_Q_RANK + MLA_KV_RANK:]
        cosm = mtab[:, :LANES]
        sinm = mtab[:, LANES:]
        tab_t = mtab.T

        cq_t = _rms(cq, gqa_ref[...], MLA_Q_RANK).T.astype(BF16)
        q2t = _dot(wqt_ref[...], cq_t)
        q_cos = gqc_ref[...] * tab_t[:HEAD_PAD]
        q_sin = gqsc_ref[...] * tab_t[HEAD_PAD:]
        for hd in range(H):
            qh = q2t[hd * HEAD_PAD:(hd + 1) * HEAD_PAD]
            qsw = q2t[(H + hd) * HEAD_PAD:(H + hd + 1) * HEAD_PAD]
            ss = jnp.sum(qh * qh, axis=0, keepdims=True)
            r = lax.rsqrt(ss * (1.0 / MLA_QK_DIM) + EPS) * scale
            qt_ref[0, hd, :, rows] = ((qh * q_cos + qsw * q_sin) * r).astype(BF16)

        ckv_n = _rms(ckv, gkva_ref[...], MLA_KV_RANK)
        kn_all = _dot(ckv_n.astype(BF16), wk_ref[...])
        kr_sq = jnp.where(lane < MLA_ROPE_DIM, kr * kr, 0.0)
        ss_rope = jnp.sum(kr_sq, axis=-1, keepdims=True)
        k_rope = jnp.where(rope_lanes, pltpu.roll(kr, MLA_NOPE_DIM, 1), 0.0)
        k_rope_sw = jnp.where(rope_lanes, pltpu.roll(kr, MLA_NOPE_DIM - MLA_ROPE_DIM, 1), 0.0)
        k_cos = gk_ref[...] * cosm
        rope_term = k_rope * k_cos + k_rope_sw * (gks_ref[...] * sinm)
        for hd in range(H):
            kn = kn_all[:, hd * HEAD_PAD:(hd + 1) * HEAD_PAD]
            ss = jnp.sum(kn * kn, axis=-1, keepdims=True) + ss_rope
            r = lax.rsqrt(ss * (1.0 / MLA_QK_DIM) + EPS)
            k_ref[0, hd, rows, :] = ((kn * k_cos + rope_term) * r).astype(BF16)

        vt = _dot(wvt_ref[...], ckv_n.T.astype(BF16))
        for hd in range(H):
            vt_ref[0, hd, 0, :, rows] = vt[hd * MLA_V_DIM:(hd + 1) * MLA_V_DIM].astype(BF16)

    n_blk = tm // FRONT_ROWS
    blocks = [slice(i * FRONT_ROWS, (i + 1) * FRONT_ROWS) for i in range(n_blk)]
    ms = [project(rows) for rows in blocks]
    mtabs = [tables(rows) for rows in blocks]
    for rows, mtab, m in zip(blocks, mtabs, ms):
        mla_prep(rows, mtab, m)


def _front(pos, rope, x, g1, win, gqa, wqt, gkva, wk, wvt, gqc, gqsc, gk, gks, tm):
    B, S, D = x.shape
    H = MLA_HEADS
    nret = 4 * RET_WIDTH
    consts = (*rope, g1, win, gqa, wqt, gkva, wk, wvt, gqc, gqsc, gk, gks)

    def const(a):
        return pl.BlockSpec(a.shape, lambda b, s: (0,) * a.ndim, pipeline_mode=pl.Buffered(1))

    specs = [const(a) for a in consts]
    return pl.pallas_call(
        _front_kernel,
        out_shape=(jax.ShapeDtypeStruct((B, S, nret), BF16),
                   jax.ShapeDtypeStruct((B, S, 2 * LANES), F32),
                   jax.ShapeDtypeStruct((B, H, HEAD_PAD, S), BF16),
                   jax.ShapeDtypeStruct((B, H, S, HEAD_PAD), BF16),
                   jax.ShapeDtypeStruct((B, H, S // tm, MLA_V_DIM, tm), BF16)),
        grid=(B, S // tm),
        in_specs=[pl.BlockSpec((1, tm, 1), lambda b, s: (b, s, 0)), *specs[:4],
                  pl.BlockSpec((1, tm, D), lambda b, s: (b, s, 0)), *specs[4:]],
        out_specs=(pl.BlockSpec((1, tm, nret), lambda b, s: (b, s, 0)),
                   pl.BlockSpec((1, tm, 2 * LANES), lambda b, s: (b, s, 0)),
                   pl.BlockSpec((1, H, HEAD_PAD, tm), lambda b, s: (b, 0, 0, s)),
                   pl.BlockSpec((1, H, tm, HEAD_PAD), lambda b, s: (b, 0, s, 0)),
                   pl.BlockSpec((1, H, 1, MLA_V_DIM, tm), lambda b, s: (b, 0, s, 0, 0))),
        compiler_params=pltpu.CompilerParams(
            dimension_semantics=("parallel", "parallel"), vmem_limit_bytes=VMEM_LIMIT),
        name="front",
    )(pos, *rope, x, g1, win, gqa, wqt, gkva, wk, wvt, gqc, gqsc, gk, gks)


def _log_sigmoid(x):
    return jnp.minimum(x, 0.0) - jnp.log(1.0 + jnp.exp(-jnp.abs(x)))


def _retention_kernel(q_ref, k_ref, v_ref, g_ref, cos_ref, sin_ref,
                      lfl_ref, lbl_ref, lfc_ref, lbc_ref, o_ref, qs, ks, kv, st, osc):
    S = q_ref.shape[1]
    C = RET_CHUNK
    n_chunks = S // C
    HD = RET_HEAD_DIM

    lane = lax.broadcasted_iota(jnp.int32, (1, LANES), 1)
    head_a = lane < HD

    r2 = lax.broadcasted_iota(jnp.int32, (LANES, LANES), 0)
    c2 = lax.broadcasted_iota(jnp.int32, (LANES, LANES), 1)
    partner = jnp.where((c2 % HD) < (HD // 2), c2 + HD // 2, c2 - HD // 2)
    swap = (r2 == partner).astype(BF16)

    def rotate(x):
        return x.astype(F32) * cos_ref[0] + _dot(x, swap) * sin_ref[0]

    qs[...] = rotate(q_ref[0]).astype(BF16)
    ks[...] = rotate(k_ref[0]).astype(BF16)
    k_scale = HD ** -0.5

    lgf_l = _log_sigmoid(lfl_ref[0])
    lgb_l = _log_sigmoid(lbl_ref[0])
    lgf_c = _log_sigmoid(lfc_ref[0])
    lgb_c = _log_sigmoid(lbc_ref[0])

    row = lax.broadcasted_iota(jnp.int32, (C, 2 * C), 0)
    col = lax.broadcasted_iota(jnp.int32, (C, 2 * C), 1) % C
    diff = (row - col).astype(F32)
    decay = k_scale * jnp.where(diff >= 0, jnp.exp(lgf_c * jnp.maximum(diff, 0.0)),
                                jnp.exp(lgb_c * jnp.maximum(-diff, 0.0)))

    idx = lax.broadcasted_iota(jnp.int32, (C, LANES), 0).astype(F32)
    zeta_f = k_scale * jnp.exp(lgf_l * (C - 1 - idx))
    xi_f = jnp.exp(lgf_l * (idx + 1))
    zeta_b = k_scale * jnp.exp(lgb_l * idx)
    xi_b = jnp.exp(lgb_l * (C - idx))
    gc_f = jnp.exp(lgf_l * C)
    gc_b = jnp.exp(lgb_l * C)

    same_head = (r2 < HD) == (c2 < HD)
    avg = jnp.where(same_head, 1.0 / HD, 0.0).astype(BF16)

    def chunk_rows(i):
        return pl.ds(pl.multiple_of(i * C, C), C)

    zero = jnp.zeros((), BF16)
    c3 = lax.broadcasted_iota(jnp.int32, (LANES, 2 * LANES), 1) % LANES
    r3 = lax.broadcasted_iota(jnp.int32, (LANES, 2 * LANES), 0)
    same_head2 = (r3 < HD) == (c3 < HD)

    def kv_body(i, carry):
        sl = chunk_rows(i)
        vc = v_ref[0, sl, :].astype(F32)
        vz = jnp.concatenate([vc * zeta_f, vc * zeta_b], axis=1).astype(BF16)
        kv[i] = jnp.where(same_head2, _dot_tn(ks[sl, :], vz), 0.0)
        return carry

    lax.fori_loop(0, n_chunks, kv_body, 0, unroll=RET_UNROLL)

    def scan_body(j, carry):
        sf, sb = carry
        jb = n_chunks - 1 - j
        st[j, :LANES, :] = sf.astype(BF16)
        st[jb, LANES:, :] = sb.astype(BF16)
        return sf * gc_f + kv[j, :, :LANES], sb * gc_b + kv[jb, :, LANES:]

    zeros = jnp.zeros((LANES, LANES), F32)
    lax.fori_loop(0, n_chunks, scan_body, (zeros, zeros), unroll=2)

    def out_body(i, carry):
        sl = chunk_rows(i)
        qc = qs[sl, :]
        kc = ks[sl, :]
        vc = v_ref[0, sl, :]
        k2 = jnp.concatenate([jnp.where(head_a, kc, zero), jnp.where(head_a, zero, kc)], axis=0)
        v2 = jnp.concatenate([jnp.where(head_a, vc, zero), jnp.where(head_a, zero, vc)], axis=0)
        p = (_dot_nt(qc, k2) * decay).astype(BF16)
        qf = qc.astype(F32)
        qx = jnp.concatenate([qf * xi_f, qf * xi_b], axis=1).astype(BF16)
        osc[sl, :] = _dot(p, v2) + _dot(qx, st[i])
        return carry

    lax.fori_loop(0, n_chunks, out_body, 0, unroll=RET_UNROLL)

    rows = min(S, GN_ROWS)
    for r in range(S // rows):
        rs = slice(r * rows, (r + 1) * rows)
        o = osc[rs, :]
        o_hi, o_lo = _hi_lo(o)
        mu = _dot(o_hi, avg) + _dot(o_lo, avg)
        d = o - mu
        var = _dot((d * d).astype(BF16), avg)
        gate = g_ref[0, rs, :].astype(F32)
        o_ref[0, rs, :] = (gate * _sigmoid(gate) * d * lax.rsqrt(var + GN_EPS)).astype(o_ref.dtype)


def _retention(ret, tables, lfl, lbl, lfc, lbc):
    B, S, _ = ret.shape
    n_pairs = RET_WIDTH // LANES
    n_chunks = S // RET_CHUNK

    def col(off):
        return pl.BlockSpec((1, S, LANES), lambda b, hp: (b, 0, off + hp))

    def per_pair(width):
        return pl.BlockSpec((1, 1, width), lambda b, hp: (hp, 0, 0))

    return pl.pallas_call(
        _retention_kernel,
        out_shape=jax.ShapeDtypeStruct((B, S, RET_WIDTH), BF16),
        grid=(B, n_pairs),
        in_specs=[
            col(0), col(n_pairs), col(2 * n_pairs), col(3 * n_pairs),
            pl.BlockSpec((1, S, LANES), lambda b, hp: (b, 0, 0)),
            pl.BlockSpec((1, S, LANES), lambda b, hp: (b, 0, 1)),
            per_pair(LANES), per_pair(LANES), per_pair(2 * RET_CHUNK), per_pair(2 * RET_CHUNK),
        ],
        out_specs=pl.BlockSpec((1, S, LANES), lambda b, hp: (b, 0, hp)),
        scratch_shapes=[
            pltpu.VMEM((S, LANES), BF16),
            pltpu.VMEM((S, LANES), BF16),
            pltpu.VMEM((n_chunks, LANES, 2 * LANES), F32),
            pltpu.VMEM((n_chunks, 2 * LANES, LANES), BF16),
            pltpu.VMEM((S, LANES), F32),
        ],
        compiler_params=pltpu.CompilerParams(
            dimension_semantics=("parallel", "parallel"), vmem_limit_bytes=VMEM_LIMIT),
        name="retention",
    )(ret, ret, ret, ret, tables, tables, lfl, lbl, lfc, lbc)


def _attention_kernel(qt_ref, k_ref, vt_ref, o_ref, s0, s1, m0, m1, acc_scr):
    i = pl.program_id(0)
    n_k, _, tk = vt_ref.shape[2:]
    n_sub, _, tqs = s0.shape

    @pl.when(i == 0)
    def _():
        s1[...] = jnp.zeros(s1.shape, F32)
        m1[...] = jnp.zeros(m1.shape, F32)

    def step(s_cur, m_cur, s_prev, m_prev_ref):
        for sub in range(n_sub):
            cols = slice(sub * tqs, (sub + 1) * tqs)
            qt = qt_ref[0, 0, :, cols]
            m_prev = m_prev_ref[sub]
            acc_scr[sub] = jnp.zeros(acc_scr.shape[1:], F32)
            m8 = jnp.full((8, tqs), -jnp.inf, F32)
            l8 = jnp.zeros((8, tqs), F32)
            for kt in range(n_k):
                rows = slice(kt * tk, (kt + 1) * tk)
                s = _dot(k_ref[0, 0, rows, :], qt)
                s_cur[sub, rows, :] = s
                m8 = jnp.maximum(m8, jnp.max(s.reshape(tk // 8, 8, tqs), axis=0))
                p = jnp.exp2(s_prev[sub, rows, :] - m_prev)
                l8 = l8 + jnp.sum(p.reshape(tk // 8, 8, tqs), axis=0)
                acc_scr[sub] += _dot(vt_ref[0, 0, kt], p.astype(BF16))
            m_cur[sub] = jnp.max(m8, axis=0, keepdims=True)
            denom = jnp.sum(l8, axis=0, keepdims=True)
            o_ref[0, :, cols] = (acc_scr[sub] / denom).astype(o_ref.dtype)

    @pl.when(i % 2 == 0)
    def _():
        step(s0, m0, s1, m1)

    @pl.when(i % 2 == 1)
    def _():
        step(s1, m1, s0, m0)


def _attention(qt, k, vt, tq):
    B, H, _, S = qt.shape
    n_k, _, tk = vt.shape[2:]
    n_q = S // tq
    n_tiles = B * H * n_q
    tqs = min(tq, ATT_SUB_TILE)
    n_sub = tq // tqs

    def tile(t):
        return t // (H * n_q), (t // n_q) % H, t % n_q

    def score_tile(i):
        return tile(jnp.minimum(i, n_tiles - 1))

    def value_tile(i):
        return tile(jnp.maximum(i - 1, 0))

    def qt_map(i):
        b, h, q = score_tile(i)
        return b, h, 0, q

    def k_map(i):
        b, h, _ = score_tile(i)
        return b, h, 0, 0

    def vt_map(i):
        b, h, _ = value_tile(i)
        return b, h, 0, 0, 0

    return pl.pallas_call(
        _attention_kernel,
        out_shape=jax.ShapeDtypeStruct((B, H * MLA_V_DIM, S), BF16),
        grid=(n_tiles + 1,),
        in_specs=[
            pl.BlockSpec((1, 1, HEAD_PAD, tq), qt_map),
            pl.BlockSpec((1, 1, S, HEAD_PAD), k_map),
            pl.BlockSpec((1, 1, n_k, MLA_V_DIM, tk), vt_map),
        ],
        out_specs=pl.BlockSpec((1, MLA_V_DIM, tq), value_tile),
        scratch_shapes=[
            pltpu.VMEM((n_sub, S, tqs), F32), pltpu.VMEM((n_sub, S, tqs), F32),
            pltpu.VMEM((n_sub, 1, tqs), F32), pltpu.VMEM((n_sub, 1, tqs), F32),
            pltpu.VMEM((n_sub, MLA_V_DIM, tqs), F32),
        ],
        compiler_params=pltpu.CompilerParams(
            dimension_semantics=("arbitrary",), vmem_limit_bytes=VMEM_LIMIT),
        name="attention",
    )(qt, k, vt)


def _out_ffn_kernel(x_ref, yr_ref, yt_ref, wo1_ref, wo2_ref, g2_ref, wg_ref, wu_ref, wd_ref, o_ref):
    x = x_ref[0]
    attn = _dot(yr_ref[0], wo1_ref[...]) + _dot_tn(yt_ref[0], wo2_ref[...])
    x1 = x + attn
    h = _rms(x1, g2_ref[...], x1.shape[-1]).astype(BF16)
    gate = _dot(h, wg_ref[...])
    up = _dot(h, wu_ref[...])
    a = (gate * _sigmoid(gate) * up).astype(BF16)
    o_ref[0] = x1 + _dot(a, wd_ref[...])


def _out_ffn(x, y_ret, y_mla_t, wo1, wo2, g2, wg, wu, wd, tm):
    B, S, D = x.shape

    def const(a):
        return pl.BlockSpec(a.shape, lambda b, s: (0,) * a.ndim, pipeline_mode=pl.Buffered(1))

    return pl.pallas_call(
        _out_ffn_kernel,
        out_shape=jax.ShapeDtypeStruct((B, S, D), F32),
        grid=(B, S // tm),
        in_specs=[
            pl.BlockSpec((1, tm, D), lambda b, s: (b, s, 0)),
            pl.BlockSpec((1, tm, RET_WIDTH), lambda b, s: (b, s, 0)),
            pl.BlockSpec((1, MLA_WIDTH, tm), lambda b, s: (b, 0, s)),
            const(wo1), const(wo2), const(g2), const(wg), const(wu), const(wd),
        ],
        out_specs=pl.BlockSpec((1, tm, D), lambda b, s: (b, s, 0)),
        compiler_params=pltpu.CompilerParams(
            dimension_semantics=("parallel", "parallel"), vmem_limit_bytes=VMEM_LIMIT),
        name="out_ffn",
    )(x, y_ret, y_mla_t, wo1, wo2, g2, wg, wu, wd)


def _rotate_half_cols(w):
    half = w.shape[-1] // 2
    return jnp.concatenate([-w[..., half:], w[..., :half]], axis=-1)


def _swap_halves(g):
    half = g.shape[-1] // 2
    return jnp.concatenate([g[..., half:], g[..., :half]], axis=-1)


def _rope_layout():
    hr = RET_HEAD_DIM // 2
    hm = MLA_ROPE_DIM // 2
    fr = ROPE_BASE ** (-jnp.arange(hr, dtype=F32) / hr)
    fm = ROPE_BASE ** (-jnp.arange(hm, dtype=F32) / hm)
    zeros = jnp.zeros((LANES - 2 * hr - 2 * hm,), F32)
    freq = jnp.concatenate([fr, fr, fm, fm, zeros])[None]
    half_pi = jnp.full((1,), math.pi / 2, F32)
    phase = jnp.concatenate([jnp.tile(half_pi, hr), jnp.zeros((hr,), F32),
                             jnp.tile(half_pi, hm), jnp.zeros((hm,), F32), zeros])[None]

    l = jnp.arange(LANES)
    src = jnp.arange(LANES)[:, None]
    ret_cos = (src == (l % hr)[None]).astype(F32)
    ret_sign = jnp.where((l % RET_HEAD_DIM) < hr, -1.0, 1.0)
    ret_sin = (src == (hr + l % hr)[None]).astype(F32) * ret_sign[None]
    on_rope = (l >= MLA_NOPE_DIM) & (l < MLA_QK_DIM)
    j = (l - MLA_NOPE_DIM) % hm
    mla_cos = ((src == (2 * hr + j)[None]) & on_rope[None]).astype(F32)
    mla_sin = ((src == (2 * hr + hm + j)[None]) & on_rope[None]).astype(F32)
    expand = jnp.concatenate([ret_cos, ret_sin, mla_cos, mla_sin], axis=1).astype(BF16)
    bias = jnp.concatenate([jnp.zeros((2 * LANES,), F32), jnp.where(on_rope, 0.0, 1.0),
                            jnp.zeros((LANES,), F32)])[None]
    return freq, phase, expand, bias


def _pad_lanes(a, width):
    return jnp.pad(a, [(0, 0)] * (a.ndim - 1) + [(0, width - a.shape[-1])])


def kernel(x, positions, norm1_g, w_in, ret_decay_logit_fwd, ret_decay_logit_bwd, q_a_norm_g, w_uq,
           kv_a_norm_g, w_ukv, q_norm_g, k_norm_g, w_o, norm2_g, w_gate, w_up, w_down):
    B, S, D = x.shape
    depth = w_in.shape[0]
    H = MLA_HEADS
    tm = min(512, S)
    tq = min(1024, S)

    rope = _rope_layout()
    pos = positions.astype(F32)[..., None]

    for layer in range(depth):
        wi = w_in[layer]
        w_rope = wi[:, -MLA_ROPE_DIM:]
        n_in = 4 * RET_WIDTH + 4 * LANES
        wi = _pad_lanes(jnp.concatenate([wi, _rotate_half_cols(w_rope)], axis=1), n_in).astype(BF16)

        wq = w_uq[layer].reshape(MLA_Q_RANK, H, MLA_QK_DIM)
        wq_sw = jnp.concatenate([jnp.zeros_like(wq[..., :MLA_NOPE_DIM]),
                                 _rotate_half_cols(wq[..., MLA_NOPE_DIM:])], axis=-1)
        wqt = jnp.concatenate([_pad_lanes(wq, HEAD_PAD).reshape(MLA_Q_RANK, H * HEAD_PAD),
                               _pad_lanes(wq_sw, HEAD_PAD).reshape(MLA_Q_RANK, H * HEAD_PAD)],
                              axis=1).T.astype(BF16)
        wkv = w_ukv[layer].reshape(MLA_KV_RANK, H, MLA_NOPE_DIM + MLA_V_DIM)
        wk = _pad_lanes(wkv[..., :MLA_NOPE_DIM], HEAD_PAD).reshape(MLA_KV_RANK, H * HEAD_PAD).astype(BF16)
        wvt = wkv[..., MLA_NOPE_DIM:].reshape(MLA_KV_RANK, H * MLA_V_DIM).T.astype(BF16)

        def head_gain(g):
            g_sw = jnp.concatenate([jnp.zeros((MLA_NOPE_DIM,), F32), _swap_halves(g[MLA_NOPE_DIM:])])
            return _pad_lanes(g, HEAD_PAD)[None], _pad_lanes(g_sw, HEAD_PAD)[None]

        gq, gqs = head_gain(q_norm_g[layer])
        gk, gks = head_gain(k_norm_g[layer])

        def per_pair(logit, reps):
            return jnp.repeat(logit.reshape(RET_HEADS // 2, 1, 2), reps, axis=-1)

        lf, lb = ret_decay_logit_fwd[layer], ret_decay_logit_bwd[layer]

        ret, tables, qt, k, vt = _front(pos, rope, x, norm1_g[layer][None], wi, q_a_norm_g[layer][None], wqt,
                                        kv_a_norm_g[layer][None], wk, wvt, gq.T, gqs.T, gk, gks, tm)
        y_ret = _retention(ret, tables, per_pair(lf, RET_HEAD_DIM), per_pair(lb, RET_HEAD_DIM),
                           per_pair(lf, RET_CHUNK), per_pair(lb, RET_CHUNK))
        y_mla_t = _attention(qt, k, vt, tq)
        wo = w_o[layer].astype(BF16)
        x = _out_ffn(x, y_ret, y_mla_t, wo[:RET_WIDTH], wo[RET_WIDTH:], norm2_g[layer][None],
                     w_gate[layer].astype(BF16), w_up[layer].astype(BF16), w_down[layer].astype(BF16), tm)
    return x
```

```python
import math

import jax
import jax.numpy as jnp
from jax import lax
from jax.experimental import pallas as pl
from jax.experimental.pallas import tpu as pltpu

RET_HEADS = 8
RET_HEAD_DIM = 64
RET_WIDTH = RET_HEADS * RET_HEAD_DIM
RET_CHUNK = 128
MLA_HEADS = 8
MLA_Q_RANK = 256
MLA_KV_RANK = 128
MLA_NOPE_DIM = 64
MLA_ROPE_DIM = 32
MLA_QK_DIM = MLA_NOPE_DIM + MLA_ROPE_DIM
MLA_V_DIM = 64
MLA_WIDTH = MLA_HEADS * MLA_V_DIM
ROPE_BASE = 10000.0
EPS = 1e-6
GN_EPS = 1e-5

LANES = 128
HEAD_PAD = LANES
RET_UNROLL = 4
GN_ROWS = 1024
ATT_SUB_TILE = 512
FRONT_ROWS = 256
VMEM_LIMIT = 56 * 1024 * 1024

F32 = jnp.float32
BF16 = jnp.bfloat16


def _dot(a, b):
    return jnp.dot(a, b, preferred_element_type=F32)


def _dot_nt(a, b):
    return lax.dot_general(a, b, (((1,), (1,)), ((), ())), preferred_element_type=F32)


def _dot_tn(a, b):
    return lax.dot_general(a, b, (((0,), (0,)), ((), ())), preferred_element_type=F32)


def _rms(x, g, n):
    ms = jnp.sum(x * x, axis=-1, keepdims=True) * (1.0 / n)
    return x * lax.rsqrt(ms + EPS) * g


def _sigmoid(x):
    return 1.0 / (1.0 + jnp.exp(-x))


def _hi_lo(x):
    hi = x.astype(BF16)
    lo = (x - hi.astype(F32)).astype(BF16)
    return hi, lo


def _front_kernel(pos_ref, freq_ref, phase_ref, expand_ref, bias_ref, x_ref, g1_ref, win_ref,
                  gqa_ref, wqt_ref, gkva_ref, wk_ref, wvt_ref, gqc_ref, gqsc_ref, gk_ref, gks_ref,
                  ret_ref, tab_ref, qt_ref, k_ref, vt_ref):
    H = MLA_HEADS
    tm = x_ref.shape[1]
    nret = ret_ref.shape[-1]
    lane = lax.broadcasted_iota(jnp.int32, (1, LANES), 1)
    rope_lanes = (lane >= MLA_NOPE_DIM) & (lane < MLA_QK_DIM)
    scale = MLA_QK_DIM ** -0.5 * math.log2(math.e)

    def tables(rows):
        tab = jnp.sin(pos_ref[0, rows, :] * freq_ref[...] + phase_ref[...])
        hi, lo = _hi_lo(tab)
        e = expand_ref[...]
        full = _dot(hi, e) + _dot(lo, e) + bias_ref[...]
        tab_ref[0, rows, :] = full[:, :2 * LANES]
        return full[:, 2 * LANES:]

    def project(rows):
        x = x_ref[0, rows, :]
        h = _rms(x, g1_ref[...], x.shape[-1]).astype(BF16)
        proj = _dot(h, win_ref[...])
        ret_ref[0, rows, :] = proj[:, :nret].astype(BF16)
        return proj[:, nret:]

    def mla_prep(rows, mtab, m):
        cq = m[:, :MLA_Q_RANK]
        ckv = m[:, MLA_Q_RANK:MLA_Q_RANK + MLA_KV_RANK]
        kr = m[:, MLA_Q_RANK + MLA_KV_RANK:]
        cosm = mtab[:, :LANES]
        sinm = mtab[:, LANES:]
        tab_t = mtab.T

        cq_t = _rms(cq, gqa_ref[...], MLA_Q_RANK).T.astype(BF16)
        q2t = _dot(wqt_ref[...], cq_t)
        q_cos = gqc_ref[...] * tab_t[:HEAD_PAD]
        q_sin = gqsc_ref[...] * tab_t[HEAD_PAD:]
        r0, r1, r2 = MLA_NOPE_DIM, MLA_NOPE_DIM + MLA_ROPE_DIM // 2, MLA_QK_DIM
        for hd in range(H):
            qh = q2t[hd * HEAD_PAD:(hd + 1) * HEAD_PAD]
            qsw = jnp.concatenate([qh[:r0], -qh[r1:r2], qh[r0:r1], qh[r2:]], axis=0)
            ss = jnp.sum(qh * qh, axis=0, keepdims=True)
            r = lax.rsqrt(ss * (1.0 / MLA_QK_DIM) + EPS) * scale
            qt_ref[0, hd, :, rows] = ((qh * q_cos + qsw * q_sin) * r).astype(BF16)

        ckv_n = _rms(ckv, gkva_ref[...], MLA_KV_RANK)
        kn_all = _dot(ckv_n.astype(BF16), wk_ref[...])
        kr_sq = jnp.where(lane < MLA_ROPE_DIM, kr * kr, 0.0)
        ss_rope = jnp.sum(kr_sq, axis=-1, keepdims=True)
        k_rope = jnp.where(rope_lanes, pltpu.roll(kr, MLA_NOPE_DIM, 1), 0.0)
        k_rope_sw = jnp.where(rope_lanes, pltpu.roll(kr, MLA_NOPE_DIM - MLA_ROPE_DIM, 1), 0.0)
        k_cos = gk_ref[...] * cosm
        rope_term = k_rope * k_cos + k_rope_sw * (gks_ref[...] * sinm)
        for hd in range(H):
            kn = kn_all[:, hd * HEAD_PAD:(hd + 1) * HEAD_PAD]
            ss = jnp.sum(kn * kn, axis=-1, keepdims=True) + ss_rope
            r = lax.rsqrt(ss * (1.0 / MLA_QK_DIM) + EPS)
            k_ref[0, hd, rows, :] = ((kn * k_cos + rope_term) * r).astype(BF16)

        vt = _dot(wvt_ref[...], ckv_n.T.astype(BF16))
        for hd in range(H):
            vt_ref[0, hd, 0, :, rows] = vt[hd * MLA_V_DIM:(hd + 1) * MLA_V_DIM].astype(BF16)

    n_blk = tm // FRONT_ROWS
    blocks = [slice(i * FRONT_ROWS, (i + 1) * FRONT_ROWS) for i in range(n_blk)]
    ms = [project(rows) for rows in blocks]
    mtabs = [tables(rows) for rows in blocks]
    for rows, mtab, m in zip(blocks, mtabs, ms):
        mla_prep(rows, mtab, m)


def _front(pos, rope, x, g1, win, gqa, wqt, gkva, wk, wvt, gqc, gqsc, gk, gks, tm):
    B, S, D = x.shape
    H = MLA_HEADS
    nret = 4 * RET_WIDTH
    consts = (*rope, g1, win, gqa, wqt, gkva, wk, wvt, gqc, gqsc, gk, gks)

    def const(a):
        return pl.BlockSpec(a.shape, lambda b, s: (0,) * a.ndim, pipeline_mode=pl.Buffered(1))

    specs = [const(a) for a in consts]
    return pl.pallas_call(
        _front_kernel,
        out_shape=(jax.ShapeDtypeStruct((B, S, nret), BF16),
                   jax.ShapeDtypeStruct((B, S, 2 * LANES), F32),
                   jax.ShapeDtypeStruct((B, H, HEAD_PAD, S), BF16),
                   jax.ShapeDtypeStruct((B, H, S, HEAD_PAD), BF16),
                   jax.ShapeDtypeStruct((B, H, S // tm, MLA_V_DIM, tm), BF16)),
        grid=(B, S // tm),
        in_specs=[pl.BlockSpec((1, tm, 1), lambda b, s: (b, s, 0)), *specs[:4],
                  pl.BlockSpec((1, tm, D), lambda b, s: (b, s, 0)), *specs[4:]],
        out_specs=(pl.BlockSpec((1, tm, nret), lambda b, s: (b, s, 0)),
                   pl.BlockSpec((1, tm, 2 * LANES), lambda b, s: (b, s, 0)),
                   pl.BlockSpec((1, H, HEAD_PAD, tm), lambda b, s: (b, 0, 0, s)),
                   pl.BlockSpec((1, H, tm, HEAD_PAD), lambda b, s: (b, 0, s, 0)),
                   pl.BlockSpec((1, H, 1, MLA_V_DIM, tm), lambda b, s: (b, 0, s, 0, 0))),
        compiler_params=pltpu.CompilerParams(
            dimension_semantics=("parallel", "parallel"), vmem_limit_bytes=VMEM_LIMIT),
        name="front",
    )(pos, *rope, x, g1, win, gqa, wqt, gkva, wk, wvt, gqc, gqsc, gk, gks)


def _log_sigmoid(x):
    return jnp.minimum(x, 0.0) - jnp.log(1.0 + jnp.exp(-jnp.abs(x)))


def _retention_kernel(q_ref, k_ref, v_ref, g_ref, cos_ref, sin_ref,
                      lfl_ref, lbl_ref, lfc_ref, lbc_ref, o_ref, qs, ks, kv, st, osc):
    S = q_ref.shape[1]
    C = RET_CHUNK
    n_chunks = S // C
    HD = RET_HEAD_DIM

    lane = lax.broadcasted_iota(jnp.int32, (1, LANES), 1)
    head_a = lane < HD

    r2 = lax.broadcasted_iota(jnp.int32, (LANES, LANES), 0)
    c2 = lax.broadcasted_iota(jnp.int32, (LANES, LANES), 1)
    rw = lax.broadcasted_iota(jnp.int32, (2 * LANES, 2 * LANES), 0)
    cw = lax.broadcasted_iota(jnp.int32, (2 * LANES, 2 * LANES), 1)
    partner = jnp.where((cw % HD) < (HD // 2), cw + HD // 2, cw - HD // 2)
    swap2 = (rw == partner).astype(BF16)

    qk = jnp.concatenate([q_ref[0], k_ref[0]], axis=1)
    qk_sw = _dot(qk, swap2)
    cos, sin = cos_ref[0], sin_ref[0]
    qs[...] = (qk[:, :LANES].astype(F32) * cos + qk_sw[:, :LANES] * sin).astype(BF16)
    ks[...] = (qk[:, LANES:].astype(F32) * cos + qk_sw[:, LANES:] * sin).astype(BF16)
    k_scale = HD ** -0.5

    lgf_l = _log_sigmoid(lfl_ref[0])
    lgb_l = _log_sigmoid(lbl_ref[0])
    lgf_c = _log_sigmoid(lfc_ref[0])
    lgb_c = _log_sigmoid(lbc_ref[0])

    row = lax.broadcasted_iota(jnp.int32, (C, 2 * C), 0)
    col = lax.broadcasted_iota(jnp.int32, (C, 2 * C), 1) % C
    diff = (row - col).astype(F32)
    decay = k_scale * jnp.where(diff >= 0, jnp.exp(lgf_c * jnp.maximum(diff, 0.0)),
                                jnp.exp(lgb_c * jnp.maximum(-diff, 0.0)))

    idx = lax.broadcasted_iota(jnp.int32, (C, LANES), 0).astype(F32)
    zeta_f = k_scale * jnp.exp(lgf_l * (C - 1 - idx))
    xi_f = jnp.exp(lgf_l * (idx + 1))
    zeta_b = k_scale * jnp.exp(lgb_l * idx)
    xi_b = jnp.exp(lgb_l * (C - idx))
    gc_f = jnp.exp(lgf_l * C)
    gc_b = jnp.exp(lgb_l * C)

    same_head = (r2 < HD) == (c2 < HD)
    avg = jnp.where(same_head, 1.0 / HD, 0.0).astype(BF16)
    avg2 = jnp.concatenate([avg, avg], axis=0)

    def chunk_rows(i):
        return pl.ds(pl.multiple_of(i * C, C), C)

    zero = jnp.zeros((), BF16)
    c3 = lax.broadcasted_iota(jnp.int32, (LANES, 2 * LANES), 1) % LANES
    r3 = lax.broadcasted_iota(jnp.int32, (LANES, 2 * LANES), 0)
    same_head2 = (r3 < HD) == (c3 < HD)

    def kv_body(i, carry):
        sl = chunk_rows(i)
        vc = v_ref[0, sl, :].astype(F32)
        vz = jnp.concatenate([vc * zeta_f, vc * zeta_b], axis=1).astype(BF16)
        kv[i] = jnp.where(same_head2, _dot_tn(ks[sl, :], vz), 0.0)
        return carry

    lax.fori_loop(0, n_chunks, kv_body, 0, unroll=RET_UNROLL)

    def scan_body(j, carry):
        sf, sb = carry
        jb = n_chunks - 1 - j
        st[j, :LANES, :] = sf.astype(BF16)
        st[jb, LANES:, :] = sb.astype(BF16)
        return sf * gc_f + kv[j, :, :LANES], sb * gc_b + kv[jb, :, LANES:]

    zeros = jnp.zeros((LANES, LANES), F32)
    lax.fori_loop(0, n_chunks, scan_body, (zeros, zeros), unroll=2)

    def out_body(i, carry):
        sl = chunk_rows(i)
        qc = qs[sl, :]
        kc = ks[sl, :]
        vc = v_ref[0, sl, :]
        k2 = jnp.concatenate([jnp.where(head_a, kc, zero), jnp.where(head_a, zero, kc)], axis=0)
        v2 = jnp.concatenate([jnp.where(head_a, vc, zero), jnp.where(head_a, zero, vc)], axis=0)
        p = (_dot_nt(qc, k2) * decay).astype(BF16)
        qf = qc.astype(F32)
        qx = jnp.concatenate([qf * xi_f, qf * xi_b], axis=1).astype(BF16)
        osc[sl, :] = _dot(p, v2) + _dot(qx, st[i])
        return carry

    lax.fori_loop(0, n_chunks, out_body, 0, unroll=RET_UNROLL)

    rows = min(S, GN_ROWS)
    for r in range(S // rows):
        rs = slice(r * rows, (r + 1) * rows)
        o = osc[rs, :]
        o_hi, o_lo = _hi_lo(o)
        mu = _dot(jnp.concatenate([o_hi, o_lo], axis=1), avg2)
        d = o - mu
        var = _dot((d * d).astype(BF16), avg)
        gate = g_ref[0, rs, :].astype(F32)
        o_ref[0, rs, :] = (gate * _sigmoid(gate) * d * lax.rsqrt(var + GN_EPS)).astype(o_ref.dtype)


def _retention(ret, tables, lfl, lbl, lfc, lbc):
    B, S, _ = ret.shape
    n_pairs = RET_WIDTH // LANES
    n_chunks = S // RET_CHUNK

    def col(off):
        return pl.BlockSpec((1, S, LANES), lambda b, hp: (b, 0, off + hp))

    def per_pair(width):
        return pl.BlockSpec((1, 1, width), lambda b, hp: (hp, 0, 0))

    return pl.pallas_call(
        _retention_kernel,
        out_shape=jax.ShapeDtypeStruct((B, S, RET_WIDTH), BF16),
        grid=(B, n_pairs),
        in_specs=[
            col(0), col(n_pairs), col(2 * n_pairs), col(3 * n_pairs),
            pl.BlockSpec((1, S, LANES), lambda b, hp: (b, 0, 0)),
            pl.BlockSpec((1, S, LANES), lambda b, hp: (b, 0, 1)),
            per_pair(LANES), per_pair(LANES), per_pair(2 * RET_CHUNK), per_pair(2 * RET_CHUNK),
        ],
        out_specs=pl.BlockSpec((1, S, LANES), lambda b, hp: (b, 0, hp)),
        scratch_shapes=[
            pltpu.VMEM((S, LANES), BF16),
            pltpu.VMEM((S, LANES), BF16),
            pltpu.VMEM((n_chunks, LANES, 2 * LANES), F32),
            pltpu.VMEM((n_chunks, 2 * LANES, LANES), BF16),
            pltpu.VMEM((S, LANES), F32),
        ],
        compiler_params=pltpu.CompilerParams(
            dimension_semantics=("parallel", "parallel"), vmem_limit_bytes=VMEM_LIMIT),
        name="retention",
    )(ret, ret, ret, ret, tables, tables, lfl, lbl, lfc, lbc)


def _attention_kernel(qt_ref, k_ref, vt_ref, o_ref, s0, s1, m0, m1, acc_scr):
    i = pl.program_id(0)
    n_k, _, tk = vt_ref.shape[2:]
    n_sub, _, tqs = s0.shape

    @pl.when(i == 0)
    def _():
        s1[...] = jnp.zeros(s1.shape, F32)
        m1[...] = jnp.zeros(m1.shape, F32)

    def step(s_cur, m_cur, s_prev, m_prev_ref):
        for sub in range(n_sub):
            cols = slice(sub * tqs, (sub + 1) * tqs)
            qt = qt_ref[0, 0, :, cols]
            m_prev = m_prev_ref[sub]
            acc_scr[sub] = jnp.zeros(acc_scr.shape[1:], F32)
            m8 = jnp.full((8, tqs), -jnp.inf, F32)
            l8 = jnp.zeros((8, tqs), F32)
            for kt in range(n_k):
                rows = slice(kt * tk, (kt + 1) * tk)
                s = _dot(k_ref[0, 0, rows, :], qt)
                s_cur[sub, rows, :] = s
                m8 = jnp.maximum(m8, jnp.max(s.reshape(tk // 8, 8, tqs), axis=0))
                p = jnp.exp2(s_prev[sub, rows, :] - m_prev)
                l8 = l8 + jnp.sum(p.reshape(tk // 8, 8, tqs), axis=0)
                acc_scr[sub] += _dot(vt_ref[0, 0, kt], p.astype(BF16))
            m_cur[sub] = jnp.max(m8, axis=0, keepdims=True)
            denom = jnp.sum(l8, axis=0, keepdims=True)
            o_ref[0, :, cols] = (acc_scr[sub] / denom).astype(o_ref.dtype)

    @pl.when(i % 2 == 0)
    def _():
        step(s0, m0, s1, m1)

    @pl.when(i % 2 == 1)
    def _():
        step(s1, m1, s0, m0)


def _attention(qt, k, vt, tq):
    B, H, _, S = qt.shape
    n_k, _, tk = vt.shape[2:]
    n_q = S // tq
    n_tiles = B * H * n_q
    tqs = min(tq, ATT_SUB_TILE)
    n_sub = tq // tqs

    def tile(t):
        return t // (H * n_q), (t // n_q) % H, t % n_q

    def score_tile(i):
        return tile(jnp.minimum(i, n_tiles - 1))

    def value_tile(i):
        return tile(jnp.maximum(i - 1, 0))

    def qt_map(i):
        b, h, q = score_tile(i)
        return b, h, 0, q

    def k_map(i):
        b, h, _ = score_tile(i)
        return b, h, 0, 0

    def vt_map(i):
        b, h, _ = value_tile(i)
        return b, h, 0, 0, 0

    return pl.pallas_call(
        _attention_kernel,
        out_shape=jax.ShapeDtypeStruct((B, H * MLA_V_DIM, S), BF16),
        grid=(n_tiles + 1,),
        in_specs=[
            pl.BlockSpec((1, 1, HEAD_PAD, tq), qt_map),
            pl.BlockSpec((1, 1, S, HEAD_PAD), k_map),
            pl.BlockSpec((1, 1, n_k, MLA_V_DIM, tk), vt_map),
        ],
        out_specs=pl.BlockSpec((1, MLA_V_DIM, tq), value_tile),
        scratch_shapes=[
            pltpu.VMEM((n_sub, S, tqs), F32), pltpu.VMEM((n_sub, S, tqs), F32),
            pltpu.VMEM((n_sub, 1, tqs), F32), pltpu.VMEM((n_sub, 1, tqs), F32),
            pltpu.VMEM((n_sub, MLA_V_DIM, tqs), F32),
        ],
        compiler_params=pltpu.CompilerParams(
            dimension_semantics=("arbitrary",), vmem_limit_bytes=VMEM_LIMIT),
        name="attention",
    )(qt, k, vt)


def _out_ffn_kernel(x_ref, yr_ref, yt_ref, wo1_ref, wo2_ref, g2_ref, wg_ref, wu_ref, wd_ref, o_ref):
    x = x_ref[0]
    attn = _dot(yr_ref[0], wo1_ref[...]) + _dot_tn(yt_ref[0], wo2_ref[...])
    x1 = x + attn
    h = _rms(x1, g2_ref[...], x1.shape[-1]).astype(BF16)
    gate = _dot(h, wg_ref[...])
    up = _dot(h, wu_ref[...])
    a = (gate * _sigmoid(gate) * up).astype(BF16)
    o_ref[0] = x1 + _dot(a, wd_ref[...])


def _out_ffn(x, y_ret, y_mla_t, wo1, wo2, g2, wg, wu, wd, tm):
    B, S, D = x.shape

    def const(a):
        return pl.BlockSpec(a.shape, lambda b, s: (0,) * a.ndim, pipeline_mode=pl.Buffered(1))

    return pl.pallas_call(
        _out_ffn_kernel,
        out_shape=jax.ShapeDtypeStruct((B, S, D), F32),
        grid=(B, S // tm),
        in_specs=[
            pl.BlockSpec((1, tm, D), lambda b, s: (b, s, 0)),
            pl.BlockSpec((1, tm, RET_WIDTH), lambda b, s: (b, s, 0)),
            pl.BlockSpec((1, MLA_WIDTH, tm), lambda b, s: (b, 0, s)),
            const(wo1), const(wo2), const(g2), const(wg), const(wu), const(wd),
        ],
        out_specs=pl.BlockSpec((1, tm, D), lambda b, s: (b, s, 0)),
        compiler_params=pltpu.CompilerParams(
            dimension_semantics=("parallel", "parallel"), vmem_limit_bytes=VMEM_LIMIT),
        name="out_ffn",
    )(x, y_ret, y_mla_t, wo1, wo2, g2, wg, wu, wd)


def _rotate_half_cols(w):
    half = w.shape[-1] // 2
    return jnp.concatenate([-w[..., half:], w[..., :half]], axis=-1)


def _swap_halves(g):
    half = g.shape[-1] // 2
    return jnp.concatenate([g[..., half:], g[..., :half]], axis=-1)


def _rope_layout():
    hr = RET_HEAD_DIM // 2
    hm = MLA_ROPE_DIM // 2
    fr = ROPE_BASE ** (-jnp.arange(hr, dtype=F32) / hr)
    fm = ROPE_BASE ** (-jnp.arange(hm, dtype=F32) / hm)
    zeros = jnp.zeros((LANES - 2 * hr - 2 * hm,), F32)
    freq = jnp.concatenate([fr, fr, fm, fm, zeros])[None]
    half_pi = jnp.full((1,), math.pi / 2, F32)
    phase = jnp.concatenate([jnp.tile(half_pi, hr), jnp.zeros((hr,), F32),
                             jnp.tile(half_pi, hm), jnp.zeros((hm,), F32), zeros])[None]

    l = jnp.arange(LANES)
    src = jnp.arange(LANES)[:, None]
    ret_cos = (src == (l % hr)[None]).astype(F32)
    ret_sign = jnp.where((l % RET_HEAD_DIM) < hr, -1.0, 1.0)
    ret_sin = (src == (hr + l % hr)[None]).astype(F32) * ret_sign[None]
    on_rope = (l >= MLA_NOPE_DIM) & (l < MLA_QK_DIM)
    j = (l - MLA_NOPE_DIM) % hm
    mla_cos = ((src == (2 * hr + j)[None]) & on_rope[None]).astype(F32)
    mla_sin = ((src == (2 * hr + hm + j)[None]) & on_rope[None]).astype(F32)
    expand = jnp.concatenate([ret_cos, ret_sin, mla_cos, mla_sin], axis=1).astype(BF16)
    bias = jnp.concatenate([jnp.zeros((2 * LANES,), F32), jnp.where(on_rope, 0.0, 1.0),
                            jnp.zeros((LANES,), F32)])[None]
    return freq, phase, expand, bias


def _pad_lanes(a, width):
    return jnp.pad(a, [(0, 0)] * (a.ndim - 1) + [(0, width - a.shape[-1])])


def kernel(x, positions, norm1_g, w_in, ret_decay_logit_fwd, ret_decay_logit_bwd, q_a_norm_g, w_uq,
           kv_a_norm_g, w_ukv, q_norm_g, k_norm_g, w_o, norm2_g, w_gate, w_up, w_down):
    B, S, D = x.shape
    depth = w_in.shape[0]
    H = MLA_HEADS
    tm = min(512, S)
    tq = min(1024, S)

    rope = _rope_layout()
    pos = positions.astype(F32)[..., None]

    for layer in range(depth):
        wi = w_in[layer]
        w_rope = wi[:, -MLA_ROPE_DIM:]
        n_in = 4 * RET_WIDTH + 4 * LANES
        wi = _pad_lanes(jnp.concatenate([wi, _rotate_half_cols(w_rope)], axis=1), n_in).astype(BF16)

        wq = w_uq[layer].reshape(MLA_Q_RANK, H, MLA_QK_DIM)
        wqt = _pad_lanes(wq, HEAD_PAD).reshape(MLA_Q_RANK, H * HEAD_PAD).T.astype(BF16)
        wkv = w_ukv[layer].reshape(MLA_KV_RANK, H, MLA_NOPE_DIM + MLA_V_DIM)
        wk = _pad_lanes(wkv[..., :MLA_NOPE_DIM], HEAD_PAD).reshape(MLA_KV_RANK, H * HEAD_PAD).astype(BF16)
        wvt = wkv[..., MLA_NOPE_DIM:].reshape(MLA_KV_RANK, H * MLA_V_DIM).T.astype(BF16)

        def head_gain(g):
            g_sw = jnp.concatenate([jnp.zeros((MLA_NOPE_DIM,), F32), _swap_halves(g[MLA_NOPE_DIM:])])
            return _pad_lanes(g, HEAD_PAD)[None], _pad_lanes(g_sw, HEAD_PAD)[None]

        gq, gqs = head_gain(q_norm_g[layer])
        gk, gks = head_gain(k_norm_g[layer])

        def per_pair(logit, reps):
            return jnp.repeat(logit.reshape(RET_HEADS // 2, 1, 2), reps, axis=-1)

        lf, lb = ret_decay_logit_fwd[layer], ret_decay_logit_bwd[layer]

        ret, tables, qt, k, vt = _front(pos, rope, x, norm1_g[layer][None], wi, q_a_norm_g[layer][None], wqt,
                                        kv_a_norm_g[layer][None], wk, wvt, gq.T, gqs.T, gk, gks, tm)
        y_ret = _retention(ret, tables, per_pair(lf, RET_HEAD_DIM), per_pair(lb, RET_HEAD_DIM),
                           per_pair(lf, RET_CHUNK), per_pair(lb, RET_CHUNK))
        y_mla_t = _attention(qt, k, vt, tq)
        wo = w_o[layer].astype(BF16)
        x = _out_ffn(x, y_ret, y_mla_t, wo[:RET_WIDTH], wo[RET_WIDTH:], norm2_g[layer][None],
                     w_gate[layer].astype(BF16), w_up[layer].astype(BF16), w_down[layer].astype(BF16), tm)
    return x
```

```python
import math

import jax
import jax.numpy as jnp
from jax import lax
from jax.experimental import pallas as pl
from jax.experimental.pallas import tpu as pltpu

RET_HEADS = 8
RET_HEAD_DIM = 64
RET_WIDTH = RET_HEADS * RET_HEAD_DIM
RET_CHUNK = 128
MLA_HEADS = 8
MLA_Q_RANK = 256
MLA_KV_RANK = 128
MLA_NOPE_DIM = 64
MLA_ROPE_DIM = 32
MLA_QK_DIM = MLA_NOPE_DIM + MLA_ROPE_DIM
MLA_V_DIM = 64
MLA_WIDTH = MLA_HEADS * MLA_V_DIM
ROPE_BASE = 10000.0
EPS = 1e-6
GN_EPS = 1e-5

LANES = 128
HEAD_PAD = LANES
RET_UNROLL = 4
GN_ROWS = 1024
ATT_SUB_TILE = 512
ATT_KEY_ROWS = 256
FRONT_ROWS = 256
VMEM_LIMIT = 56 * 1024 * 1024

F32 = jnp.float32
BF16 = jnp.bfloat16


def _dot(a, b):
    return jnp.dot(a, b, preferred_element_type=F32)


def _dot_nt(a, b):
    return lax.dot_general(a, b, (((1,), (1,)), ((), ())), preferred_element_type=F32)


def _dot_tn(a, b):
    return lax.dot_general(a, b, (((0,), (0,)), ((), ())), preferred_element_type=F32)


def _rms(x, g, n):
    ms = jnp.sum(x * x, axis=-1, keepdims=True) * (1.0 / n)
    return x * lax.rsqrt(ms + EPS) * g


def _sigmoid(x):
    return 1.0 / (1.0 + jnp.exp(-x))


def _hi_lo(x):
    hi = x.astype(BF16)
    lo = (x - hi.astype(F32)).astype(BF16)
    return hi, lo


def _front_kernel(pos_ref, freq_ref, phase_ref, expand_ref, bias_ref, x_ref, g1_ref, win_ref,
                  gqa_ref, wqt_ref, gkva_ref, wk_ref, wvt_ref, gqc_ref, gqsc_ref, gk_ref, gks_ref,
                  ret_ref, tab_ref, qt_ref, k_ref, vt_ref):
    H = MLA_HEADS
    tm = x_ref.shape[1]
    nret = ret_ref.shape[-1]
    lane = lax.broadcasted_iota(jnp.int32, (1, LANES), 1)
    rope_lanes = (lane >= MLA_NOPE_DIM) & (lane < MLA_QK_DIM)
    scale = MLA_QK_DIM ** -0.5 * math.log2(math.e)

    def tables(rows):
        tab = jnp.sin(pos_ref[0, rows, :] * freq_ref[...] + phase_ref[...])
        hi, lo = _hi_lo(tab)
        e = expand_ref[...]
        full = _dot(hi, e) + _dot(lo, e) + bias_ref[...]
        tab_ref[0, rows, :] = full[:, :2 * LANES]
        return full[:, 2 * LANES:]

    def project(rows):
        x = x_ref[0, rows, :]
        h = _rms(x, g1_ref[...], x.shape[-1]).astype(BF16)
        proj = _dot(h, win_ref[...])
        ret_ref[0, rows, :] = proj[:, :nret].astype(BF16)
        return proj[:, nret:]

    def mla_prep(rows, mtab, m):
        cq = m[:, :MLA_Q_RANK]
        ckv = m[:, MLA_Q_RANK:MLA_Q_RANK + MLA_KV_RANK]
        kr = m[:, MLA_Q_RANK + MLA_KV_RANK:]
        cosm = mtab[:, :LANES]
        sinm = mtab[:, LANES:]
        tab_t = mtab.T

        cq_t = _rms(cq, gqa_ref[...], MLA_Q_RANK).T.astype(BF16)
        q2t = _dot(wqt_ref[...], cq_t)
        q_cos = gqc_ref[...] * tab_t[:HEAD_PAD]
        q_sin = gqsc_ref[...] * tab_t[HEAD_PAD:]
        r0, r1, r2 = MLA_NOPE_DIM, MLA_NOPE_DIM + MLA_ROPE_DIM // 2, MLA_QK_DIM
        for hd in range(H):
            qh = q2t[hd * HEAD_PAD:(hd + 1) * HEAD_PAD]
            qsw = jnp.concatenate([qh[:r0], -qh[r1:r2], qh[r0:r1], qh[r2:]], axis=0)
            ss = jnp.sum(qh * qh, axis=0, keepdims=True)
            r = lax.rsqrt(ss * (1.0 / MLA_QK_DIM) + EPS) * scale
            qt_ref[0, hd, :, rows] = ((qh * q_cos + qsw * q_sin) * r).astype(BF16)

        ckv_n = _rms(ckv, gkva_ref[...], MLA_KV_RANK)
        kn_all = _dot(ckv_n.astype(BF16), wk_ref[...])
        kr_sq = jnp.where(lane < MLA_ROPE_DIM, kr * kr, 0.0)
        ss_rope = jnp.sum(kr_sq, axis=-1, keepdims=True)
        k_rope = jnp.where(rope_lanes, pltpu.roll(kr, MLA_NOPE_DIM, 1), 0.0)
        k_rope_sw = jnp.where(rope_lanes, pltpu.roll(kr, MLA_NOPE_DIM - MLA_ROPE_DIM, 1), 0.0)
        k_cos = gk_ref[...] * cosm
        rope_term = k_rope * k_cos + k_rope_sw * (gks_ref[...] * sinm)
        for hd in range(H):
            kn = kn_all[:, hd * HEAD_PAD:(hd + 1) * HEAD_PAD]
            ss = jnp.sum(kn * kn, axis=-1, keepdims=True) + ss_rope
            r = lax.rsqrt(ss * (1.0 / MLA_QK_DIM) + EPS)
            k_ref[0, hd, rows, :] = ((kn * k_cos + rope_term) * r).astype(BF16)

        vt = _dot(wvt_ref[...], ckv_n.T.astype(BF16))
        for hd in range(H):
            vt_ref[0, hd, 0, :, rows] = vt[hd * MLA_V_DIM:(hd + 1) * MLA_V_DIM].astype(BF16)

    n_blk = tm // FRONT_ROWS
    blocks = [slice(i * FRONT_ROWS, (i + 1) * FRONT_ROWS) for i in range(n_blk)]
    ms = [project(rows) for rows in blocks]
    mtabs = [tables(rows) for rows in blocks]
    for rows, mtab, m in zip(blocks, mtabs, ms):
        mla_prep(rows, mtab, m)


def _front(pos, rope, x, g1, win, gqa, wqt, gkva, wk, wvt, gqc, gqsc, gk, gks, tm):
    B, S, D = x.shape
    H = MLA_HEADS
    nret = 4 * RET_WIDTH
    consts = (*rope, g1, win, gqa, wqt, gkva, wk, wvt, gqc, gqsc, gk, gks)

    def const(a):
        return pl.BlockSpec(a.shape, lambda b, s: (0,) * a.ndim, pipeline_mode=pl.Buffered(1))

    specs = [const(a) for a in consts]
    return pl.pallas_call(
        _front_kernel,
        out_shape=(jax.ShapeDtypeStruct((B, S, nret), BF16),
                   jax.ShapeDtypeStruct((B, S, 2 * LANES), F32),
                   jax.ShapeDtypeStruct((B, H, HEAD_PAD, S), BF16),
                   jax.ShapeDtypeStruct((B, H, S, HEAD_PAD), BF16),
                   jax.ShapeDtypeStruct((B, H, S // tm, MLA_V_DIM, tm), BF16)),
        grid=(B, S // tm),
        in_specs=[pl.BlockSpec((1, tm, 1), lambda b, s: (b, s, 0)), *specs[:4],
                  pl.BlockSpec((1, tm, D), lambda b, s: (b, s, 0)), *specs[4:]],
        out_specs=(pl.BlockSpec((1, tm, nret), lambda b, s: (b, s, 0)),
                   pl.BlockSpec((1, tm, 2 * LANES), lambda b, s: (b, s, 0)),
                   pl.BlockSpec((1, H, HEAD_PAD, tm), lambda b, s: (b, 0, 0, s)),
                   pl.BlockSpec((1, H, tm, HEAD_PAD), lambda b, s: (b, 0, s, 0)),
                   pl.BlockSpec((1, H, 1, MLA_V_DIM, tm), lambda b, s: (b, 0, s, 0, 0))),
        compiler_params=pltpu.CompilerParams(
            dimension_semantics=("parallel", "parallel"), vmem_limit_bytes=VMEM_LIMIT),
        name="front",
    )(pos, *rope, x, g1, win, gqa, wqt, gkva, wk, wvt, gqc, gqsc, gk, gks)


def _log_sigmoid(x):
    return jnp.minimum(x, 0.0) - jnp.log(1.0 + jnp.exp(-jnp.abs(x)))


def _retention_kernel(q_ref, k_ref, v_ref, g_ref, cos_ref, sin_ref,
                      lfl_ref, lbl_ref, lfc_ref, lbc_ref, o_ref, qs, ks, kv, st, osc):
    S = q_ref.shape[1]
    C = RET_CHUNK
    n_chunks = S // C
    HD = RET_HEAD_DIM

    lane = lax.broadcasted_iota(jnp.int32, (1, LANES), 1)
    head_a = lane < HD

    r2 = lax.broadcasted_iota(jnp.int32, (LANES, LANES), 0)
    c2 = lax.broadcasted_iota(jnp.int32, (LANES, LANES), 1)
    rw = lax.broadcasted_iota(jnp.int32, (2 * LANES, 2 * LANES), 0)
    cw = lax.broadcasted_iota(jnp.int32, (2 * LANES, 2 * LANES), 1)
    partner = jnp.where((cw % HD) < (HD // 2), cw + HD // 2, cw - HD // 2)
    swap2 = (rw == partner).astype(BF16)

    qk = jnp.concatenate([q_ref[0], k_ref[0]], axis=1)
    qk_sw = _dot(qk, swap2)
    cos, sin = cos_ref[0], sin_ref[0]
    qs[...] = (qk[:, :LANES].astype(F32) * cos + qk_sw[:, :LANES] * sin).astype(BF16)
    ks[...] = (qk[:, LANES:].astype(F32) * cos + qk_sw[:, LANES:] * sin).astype(BF16)
    k_scale = HD ** -0.5

    lgf_l = _log_sigmoid(lfl_ref[0])
    lgb_l = _log_sigmoid(lbl_ref[0])
    lgf_c = _log_sigmoid(lfc_ref[0])
    lgb_c = _log_sigmoid(lbc_ref[0])

    row = lax.broadcasted_iota(jnp.int32, (C, 2 * C), 0)
    col = lax.broadcasted_iota(jnp.int32, (C, 2 * C), 1) % C
    diff = (row - col).astype(F32)
    decay = k_scale * jnp.where(diff >= 0, jnp.exp(lgf_c * jnp.maximum(diff, 0.0)),
                                jnp.exp(lgb_c * jnp.maximum(-diff, 0.0)))

    idx = lax.broadcasted_iota(jnp.int32, (C, LANES), 0).astype(F32)
    zeta_f = k_scale * jnp.exp(lgf_l * (C - 1 - idx))
    xi_f = jnp.exp(lgf_l * (idx + 1))
    zeta_b = k_scale * jnp.exp(lgb_l * idx)
    xi_b = jnp.exp(lgb_l * (C - idx))
    gc_f = jnp.exp(lgf_l * C)
    gc_b = jnp.exp(lgb_l * C)

    same_head = (r2 < HD) == (c2 < HD)
    avg = jnp.where(same_head, 1.0 / HD, 0.0).astype(BF16)
    avg2 = jnp.concatenate([avg, avg], axis=0)

    def chunk_rows(i):
        return pl.ds(pl.multiple_of(i * C, C), C)

    zero = jnp.zeros((), BF16)
    c3 = lax.broadcasted_iota(jnp.int32, (LANES, 2 * LANES), 1) % LANES
    r3 = lax.broadcasted_iota(jnp.int32, (LANES, 2 * LANES), 0)
    same_head2 = (r3 < HD) == (c3 < HD)

    def kv_body(i, carry):
        sl = chunk_rows(i)
        vc = v_ref[0, sl, :].astype(F32)
        vz = jnp.concatenate([vc * zeta_f, vc * zeta_b], axis=1).astype(BF16)
        kv[i] = jnp.where(same_head2, _dot_tn(ks[sl, :], vz), 0.0)
        return carry

    lax.fori_loop(0, n_chunks, kv_body, 0, unroll=RET_UNROLL)

    def scan_body(j, carry):
        sf, sb = carry
        jb = n_chunks - 1 - j
        st[j, :LANES, :] = sf.astype(BF16)
        st[jb, LANES:, :] = sb.astype(BF16)
        return sf * gc_f + kv[j, :, :LANES], sb * gc_b + kv[jb, :, LANES:]

    zeros = jnp.zeros((LANES, LANES), F32)
    lax.fori_loop(0, n_chunks, scan_body, (zeros, zeros), unroll=2)

    def out_body(i, carry):
        sl = chunk_rows(i)
        qc = qs[sl, :]
        kc = ks[sl, :]
        vc = v_ref[0, sl, :]
        k2 = jnp.concatenate([jnp.where(head_a, kc, zero), jnp.where(head_a, zero, kc)], axis=0)
        v2 = jnp.concatenate([jnp.where(head_a, vc, zero), jnp.where(head_a, zero, vc)], axis=0)
        p = (_dot_nt(qc, k2) * decay).astype(BF16)
        qf = qc.astype(F32)
        qx = jnp.concatenate([qf * xi_f, qf * xi_b], axis=1).astype(BF16)
        osc[sl, :] = _dot(p, v2) + _dot(qx, st[i])
        return carry

    lax.fori_loop(0, n_chunks, out_body, 0, unroll=RET_UNROLL)

    rows = min(S, GN_ROWS)
    for r in range(S // rows):
        rs = slice(r * rows, (r + 1) * rows)
        o = osc[rs, :]
        o_hi, o_lo = _hi_lo(o)
        mu = _dot(jnp.concatenate([o_hi, o_lo], axis=1), avg2)
        d = o - mu
        var = _dot((d * d).astype(BF16), avg)
        gate = g_ref[0, rs, :].astype(F32)
        o_ref[0, rs, :] = (gate * _sigmoid(gate) * d * lax.rsqrt(var + GN_EPS)).astype(o_ref.dtype)


def _retention(ret, tables, lfl, lbl, lfc, lbc):
    B, S, _ = ret.shape
    n_pairs = RET_WIDTH // LANES
    n_chunks = S // RET_CHUNK

    def col(off):
        return pl.BlockSpec((1, S, LANES), lambda b, hp: (b, 0, off + hp))

    def per_pair(width):
        return pl.BlockSpec((1, 1, width), lambda b, hp: (hp, 0, 0))

    return pl.pallas_call(
        _retention_kernel,
        out_shape=jax.ShapeDtypeStruct((B, S, RET_WIDTH), BF16),
        grid=(B, n_pairs),
        in_specs=[
            col(0), col(n_pairs), col(2 * n_pairs), col(3 * n_pairs),
            pl.BlockSpec((1, S, LANES), lambda b, hp: (b, 0, 0)),
            pl.BlockSpec((1, S, LANES), lambda b, hp: (b, 0, 1)),
            per_pair(LANES), per_pair(LANES), per_pair(2 * RET_CHUNK), per_pair(2 * RET_CHUNK),
        ],
        out_specs=pl.BlockSpec((1, S, LANES), lambda b, hp: (b, 0, hp)),
        scratch_shapes=[
            pltpu.VMEM((S, LANES), BF16),
            pltpu.VMEM((S, LANES), BF16),
            pltpu.VMEM((n_chunks, LANES, 2 * LANES), F32),
            pltpu.VMEM((n_chunks, 2 * LANES, LANES), BF16),
            pltpu.VMEM((S, LANES), F32),
        ],
        compiler_params=pltpu.CompilerParams(
            dimension_semantics=("parallel", "parallel"), vmem_limit_bytes=VMEM_LIMIT),
        name="retention",
    )(ret, ret, ret, ret, tables, tables, lfl, lbl, lfc, lbc)


def _attention_kernel(qt_ref, k_ref, vt_ref, o_ref, s0, s1, m0, m1, acc_scr):
    i = pl.program_id(0)
    tk = vt_ref.shape[4]
    n_sub, S, tqs = s0.shape
    kb = min(ATT_KEY_ROWS, tk)

    @pl.when(i == 0)
    def _():
        s1[...] = jnp.zeros(s1.shape, F32)
        m1[...] = jnp.zeros(m1.shape, F32)

    def step(s_cur, m_cur, s_prev, m_prev_ref):
        for sub in range(n_sub):
            cols = slice(sub * tqs, (sub + 1) * tqs)
            qt = qt_ref[0, 0, :, cols]
            m_prev = m_prev_ref[sub]
            acc_scr[sub] = jnp.zeros(acc_scr.shape[1:], F32)
            m8 = jnp.full((8, tqs), -jnp.inf, F32)
            l8 = jnp.zeros((8, tqs), F32)
            for kr in range(S // kb):
                rows = slice(kr * kb, (kr + 1) * kb)
                s = _dot(k_ref[0, 0, rows, :], qt)
                s_cur[sub, rows, :] = s
                m8 = jnp.maximum(m8, jnp.max(s.reshape(kb // 8, 8, tqs), axis=0))
                p = jnp.exp2(s_prev[sub, rows, :] - m_prev)
                l8 = l8 + jnp.sum(p.reshape(kb // 8, 8, tqs), axis=0)
                kt, off = divmod(kr * kb, tk)
                acc_scr[sub] += _dot(vt_ref[0, 0, kt, :, off:off + kb], p.astype(BF16))
            m_cur[sub] = jnp.max(m8, axis=0, keepdims=True)
            denom = jnp.sum(l8, axis=0, keepdims=True)
            o_ref[0, :, cols] = (acc_scr[sub] / denom).astype(o_ref.dtype)

    @pl.when(i % 2 == 0)
    def _():
        step(s0, m0, s1, m1)

    @pl.when(i % 2 == 1)
    def _():
        step(s1, m1, s0, m0)


def _attention(qt, k, vt, tq):
    B, H, _, S = qt.shape
    n_k, _, tk = vt.shape[2:]
    n_q = S // tq
    n_tiles = B * H * n_q
    tqs = min(tq, ATT_SUB_TILE)
    n_sub = tq // tqs

    def tile(t):
        return t // (H * n_q), (t // n_q) % H, t % n_q

    def score_tile(i):
        return tile(jnp.minimum(i, n_tiles - 1))

    def value_tile(i):
        return tile(jnp.maximum(i - 1, 0))

    def qt_map(i):
        b, h, q = score_tile(i)
        return b, h, 0, q

    def k_map(i):
        b, h, _ = score_tile(i)
        return b, h, 0, 0

    def vt_map(i):
        b, h, _ = value_tile(i)
        return b, h, 0, 0, 0

    return pl.pallas_call(
        _attention_kernel,
        out_shape=jax.ShapeDtypeStruct((B, H * MLA_V_DIM, S), BF16),
        grid=(n_tiles + 1,),
        in_specs=[
            pl.BlockSpec((1, 1, HEAD_PAD, tq), qt_map),
            pl.BlockSpec((1, 1, S, HEAD_PAD), k_map),
            pl.BlockSpec((1, 1, n_k, MLA_V_DIM, tk), vt_map),
        ],
        out_specs=pl.BlockSpec((1, MLA_V_DIM, tq), value_tile),
        scratch_shapes=[
            pltpu.VMEM((n_sub, S, tqs), F32), pltpu.VMEM((n_sub, S, tqs), F32),
            pltpu.VMEM((n_sub, 1, tqs), F32), pltpu.VMEM((n_sub, 1, tqs), F32),
            pltpu.VMEM((n_sub, MLA_V_DIM, tqs), F32),
        ],
        compiler_params=pltpu.CompilerParams(
            dimension_semantics=("arbitrary",), vmem_limit_bytes=VMEM_LIMIT),
        name="attention",
    )(qt, k, vt)


def _out_ffn_kernel(x_ref, yr_ref, yt_ref, wo1_ref, wo2_ref, g2_ref, wg_ref, wu_ref, wd_ref, o_ref):
    x = x_ref[0]
    attn = _dot(yr_ref[0], wo1_ref[...]) + _dot_tn(yt_ref[0], wo2_ref[...])
    x1 = x + attn
    h = _rms(x1, g2_ref[...], x1.shape[-1]).astype(BF16)
    gate = _dot(h, wg_ref[...])
    up = _dot(h, wu_ref[...])
    a = (gate * _sigmoid(gate) * up).astype(BF16)
    o_ref[0] = x1 + _dot(a, wd_ref[...])


def _out_ffn(x, y_ret, y_mla_t, wo1, wo2, g2, wg, wu, wd, tm):
    B, S, D = x.shape

    def const(a):
        return pl.BlockSpec(a.shape, lambda b, s: (0,) * a.ndim, pipeline_mode=pl.Buffered(1))

    return pl.pallas_call(
        _out_ffn_kernel,
        out_shape=jax.ShapeDtypeStruct((B, S, D), F32),
        grid=(B, S // tm),
        in_specs=[
            pl.BlockSpec((1, tm, D), lambda b, s: (b, s, 0)),
            pl.BlockSpec((1, tm, RET_WIDTH), lambda b, s: (b, s, 0)),
            pl.BlockSpec((1, MLA_WIDTH, tm), lambda b, s: (b, 0, s)),
            const(wo1), const(wo2), const(g2), const(wg), const(wu), const(wd),
        ],
        out_specs=pl.BlockSpec((1, tm, D), lambda b, s: (b, s, 0)),
        compiler_params=pltpu.CompilerParams(
            dimension_semantics=("parallel", "parallel"), vmem_limit_bytes=VMEM_LIMIT),
        name="out_ffn",
    )(x, y_ret, y_mla_t, wo1, wo2, g2, wg, wu, wd)


def _rotate_half_cols(w):
    half = w.shape[-1] // 2
    return jnp.concatenate([-w[..., half:], w[..., :half]], axis=-1)


def _swap_halves(g):
    half = g.shape[-1] // 2
    return jnp.concatenate([g[..., half:], g[..., :half]], axis=-1)


def _rope_layout():
    hr = RET_HEAD_DIM // 2
    hm = MLA_ROPE_DIM // 2
    fr = ROPE_BASE ** (-jnp.arange(hr, dtype=F32) / hr)
    fm = ROPE_BASE ** (-jnp.arange(hm, dtype=F32) / hm)
    zeros = jnp.zeros((LANES - 2 * hr - 2 * hm,), F32)
    freq = jnp.concatenate([fr, fr, fm, fm, zeros])[None]
    half_pi = jnp.full((1,), math.pi / 2, F32)
    phase = jnp.concatenate([jnp.tile(half_pi, hr), jnp.zeros((hr,), F32),
                             jnp.tile(half_pi, hm), jnp.zeros((hm,), F32), zeros])[None]

    l = jnp.arange(LANES)
    src = jnp.arange(LANES)[:, None]
    ret_cos = (src == (l % hr)[None]).astype(F32)
    ret_sign = jnp.where((l % RET_HEAD_DIM) < hr, -1.0, 1.0)
    ret_sin = (src == (hr + l % hr)[None]).astype(F32) * ret_sign[None]
    on_rope = (l >= MLA_NOPE_DIM) & (l < MLA_QK_DIM)
    j = (l - MLA_NOPE_DIM) % hm
    mla_cos = ((src == (2 * hr + j)[None]) & on_rope[None]).astype(F32)
    mla_sin = ((src == (2 * hr + hm + j)[None]) & on_rope[None]).astype(F32)
    expand = jnp.concatenate([ret_cos, ret_sin, mla_cos, mla_sin], axis=1).astype(BF16)
    bias = jnp.concatenate([jnp.zeros((2 * LANES,), F32), jnp.where(on_rope, 0.0, 1.0),
                            jnp.zeros((LANES,), F32)])[None]
    return freq, phase, expand, bias


def _pad_lanes(a, width):
    return jnp.pad(a, [(0, 0)] * (a.ndim - 1) + [(0, width - a.shape[-1])])


def kernel(x, positions, norm1_g, w_in, ret_decay_logit_fwd, ret_decay_logit_bwd, q_a_norm_g, w_uq,
           kv_a_norm_g, w_ukv, q_norm_g, k_norm_g, w_o, norm2_g, w_gate, w_up, w_down):
    B, S, D = x.shape
    depth = w_in.shape[0]
    H = MLA_HEADS
    tm = min(512, S)
    tq = min(1024, S)

    rope = _rope_layout()
    pos = positions.astype(F32)[..., None]

    for layer in range(depth):
        wi = w_in[layer]
        w_rope = wi[:, -MLA_ROPE_DIM:]
        n_in = 4 * RET_WIDTH + 4 * LANES
        wi = _pad_lanes(jnp.concatenate([wi, _rotate_half_cols(w_rope)], axis=1), n_in).astype(BF16)

        wq = w_uq[layer].reshape(MLA_Q_RANK, H, MLA_QK_DIM)
        wqt = _pad_lanes(wq, HEAD_PAD).reshape(MLA_Q_RANK, H * HEAD_PAD).T.astype(BF16)
        wkv = w_ukv[layer].reshape(MLA_KV_RANK, H, MLA_NOPE_DIM + MLA_V_DIM)
        wk = _pad_lanes(wkv[..., :MLA_NOPE_DIM], HEAD_PAD).reshape(MLA_KV_RANK, H * HEAD_PAD).astype(BF16)
        wvt = wkv[..., MLA_NOPE_DIM:].reshape(MLA_KV_RANK, H * MLA_V_DIM).T.astype(BF16)

        def head_gain(g):
            g_sw = jnp.concatenate([jnp.zeros((MLA_NOPE_DIM,), F32), _swap_halves(g[MLA_NOPE_DIM:])])
            return _pad_lanes(g, HEAD_PAD)[None], _pad_lanes(g_sw, HEAD_PAD)[None]

        gq, gqs = head_gain(q_norm_g[layer])
        gk, gks = head_gain(k_norm_g[layer])

        def per_pair(logit, reps):
            return jnp.repeat(logit.reshape(RET_HEADS // 2, 1, 2), reps, axis=-1)

        lf, lb = ret_decay_logit_fwd[layer], ret_decay_logit_bwd[layer]

        ret, tables, qt, k, vt = _front(pos, rope, x, norm1_g[layer][None], wi, q_a_norm_g[layer][None], wqt,
                                        kv_a_norm_g[layer][None], wk, wvt, gq.T, gqs.T, gk, gks, tm)
        y_ret = _retention(ret, tables, per_pair(lf, RET_HEAD_DIM), per_pair(lb, RET_HEAD_DIM),
                           per_pair(lf, RET_CHUNK), per_pair(lb, RET_CHUNK))
        y_mla_t = _attention(qt, k, vt, tq)
        wo = w_o[layer].astype(BF16)
        x = _out_ffn(x, y_ret, y_mla_t, wo[:RET_WIDTH], wo[RET_WIDTH:], norm2_g[layer][None],
                     w_gate[layer].astype(BF16), w_up[layer].astype(BF16), w_down[layer].astype(BF16), tm)
    return x
```

```python
import math

import jax
import jax.numpy as jnp
from jax import lax
from jax.experimental import pallas as pl
from jax.experimental.pallas import tpu as pltpu

RET_HEADS = 8
RET_HEAD_DIM = 64
RET_WIDTH = RET_HEADS * RET_HEAD_DIM
RET_CHUNK = 128
MLA_HEADS = 8
MLA_Q_RANK = 256
MLA_KV_RANK = 128
MLA_NOPE_DIM = 64
MLA_ROPE_DIM = 32
MLA_QK_DIM = MLA_NOPE_DIM + MLA_ROPE_DIM
MLA_V_DIM = 64
MLA_WIDTH = MLA_HEADS * MLA_V_DIM
ROPE_BASE = 10000.0
EPS = 1e-6
GN_EPS = 1e-5

LANES = 128
HEAD_PAD = LANES
RET_UNROLL = 32
GN_ROWS = 4096
ATT_SUB_TILE = 512
ATT_KEY_ROWS = 256
FRONT_ROWS = 128
VMEM_LIMIT = 56 * 1024 * 1024

F32 = jnp.float32
BF16 = jnp.bfloat16


def _dot(a, b):
    return jnp.dot(a, b, preferred_element_type=F32)


def _dot_nt(a, b):
    return lax.dot_general(a, b, (((1,), (1,)), ((), ())), preferred_element_type=F32)


def _dot_tn(a, b):
    return lax.dot_general(a, b, (((0,), (0,)), ((), ())), preferred_element_type=F32)


def _rms(x, g, n):
    ms = jnp.sum(x * x, axis=-1, keepdims=True) * (1.0 / n)
    return x * lax.rsqrt(ms + EPS) * g


def _sigmoid(x):
    return 1.0 / (1.0 + jnp.exp(-x))


def _hi_lo(x):
    hi = x.astype(BF16)
    lo = (x - hi.astype(F32)).astype(BF16)
    return hi, lo


def _front_kernel(pos_ref, freq_ref, phase_ref, expand_ref, bias_ref, x_ref, g1_ref, win_ref,
                  gqa_ref, wqt_ref, gkva_ref, wk_ref, wvt_ref, gqc_ref, gqsc_ref, gk_ref, gks_ref,
                  ret_ref, tab_ref, qt_ref, k_ref, vt_ref):
    H = MLA_HEADS
    tm = x_ref.shape[1]
    nret = ret_ref.shape[-1]
    lane = lax.broadcasted_iota(jnp.int32, (1, LANES), 1)
    rope_lanes = (lane >= MLA_NOPE_DIM) & (lane < MLA_QK_DIM)
    scale = MLA_QK_DIM ** -0.5 * math.log2(math.e)

    def tables(rows):
        tab = jnp.sin(pos_ref[0, rows, :] * freq_ref[...] + phase_ref[...])
        hi, lo = _hi_lo(tab)
        e = expand_ref[...]
        full = _dot(hi, e) + _dot(lo, e) + bias_ref[...]
        tab_ref[0, rows, :] = full[:, :2 * LANES]
        return full[:, 2 * LANES:]

    def project(rows):
        x = x_ref[0, rows, :]
        h = _rms(x, g1_ref[...], x.shape[-1]).astype(BF16)
        proj = _dot(h, win_ref[...])
        ret_ref[0, rows, :] = proj[:, :nret].astype(BF16)
        return proj[:, nret:]

    def mla_prep(rows, mtab, m):
        cq = m[:, :MLA_Q_RANK]
        ckv = m[:, MLA_Q_RANK:MLA_Q_RANK + MLA_KV_RANK]
        kr = m[:, MLA_Q_RANK + MLA_KV_RANK:]
        cosm = mtab[:, :LANES]
        sinm = mtab[:, LANES:]
        tab_t = mtab.T

        cq_t = _rms(cq, gqa_ref[...], MLA_Q_RANK).T.astype(BF16)
        q2t = _dot(wqt_ref[...], cq_t)
        q_cos = gqc_ref[...] * tab_t[:HEAD_PAD]
        q_sin = gqsc_ref[...] * tab_t[HEAD_PAD:]
        r0, r1, r2 = MLA_NOPE_DIM, MLA_NOPE_DIM + MLA_ROPE_DIM // 2, MLA_QK_DIM
        for hd in range(H):
            qh = q2t[hd * HEAD_PAD:(hd + 1) * HEAD_PAD]
            qsw = jnp.concatenate([qh[:r0], -qh[r1:r2], qh[r0:r1], qh[r2:]], axis=0)
            ss = jnp.sum(qh * qh, axis=0, keepdims=True)
            r = lax.rsqrt(ss * (1.0 / MLA_QK_DIM) + EPS) * scale
            qt_ref[0, hd, :, rows] = ((qh * q_cos + qsw * q_sin) * r).astype(BF16)

        ckv_n = _rms(ckv, gkva_ref[...], MLA_KV_RANK)
        kn_all = _dot(ckv_n.astype(BF16), wk_ref[...])
        kr_sq = jnp.where(lane < MLA_ROPE_DIM, kr * kr, 0.0)
        ss_rope = jnp.sum(kr_sq, axis=-1, keepdims=True)
        k_rope = jnp.where(rope_lanes, pltpu.roll(kr, MLA_NOPE_DIM, 1), 0.0)
        k_rope_sw = jnp.where(rope_lanes, pltpu.roll(kr, MLA_NOPE_DIM - MLA_ROPE_DIM, 1), 0.0)
        k_cos = gk_ref[...] * cosm
        rope_term = k_rope * k_cos + k_rope_sw * (gks_ref[...] * sinm)
        for hd in range(H):
            kn = kn_all[:, hd * HEAD_PAD:(hd + 1) * HEAD_PAD]
            ss = jnp.sum(kn * kn, axis=-1, keepdims=True) + ss_rope
            r = lax.rsqrt(ss * (1.0 / MLA_QK_DIM) + EPS)
            k_ref[0, hd, rows, :] = ((kn * k_cos + rope_term) * r).astype(BF16)

        vt = _dot(wvt_ref[...], ckv_n.T.astype(BF16))
        for hd in range(H):
            vt_ref[0, hd, 0, :, rows] = vt[hd * MLA_V_DIM:(hd + 1) * MLA_V_DIM].astype(BF16)

    n_blk = tm // FRONT_ROWS
    blocks = [slice(i * FRONT_ROWS, (i + 1) * FRONT_ROWS) for i in range(n_blk)]
    ms = [project(rows) for rows in blocks]
    mtabs = [tables(rows) for rows in blocks]
    for rows, mtab, m in zip(blocks, mtabs, ms):
        mla_prep(rows, mtab, m)


def _front(pos, rope, x, g1, win, gqa, wqt, gkva, wk, wvt, gqc, gqsc, gk, gks, tm):
    B, S, D = x.shape
    H = MLA_HEADS
    nret = 4 * RET_WIDTH
    consts = (*rope, g1, win, gqa, wqt, gkva, wk, wvt, gqc, gqsc, gk, gks)

    def const(a):
        return pl.BlockSpec(a.shape, lambda b, s: (0,) * a.ndim, pipeline_mode=pl.Buffered(1))

    specs = [const(a) for a in consts]
    return pl.pallas_call(
        _front_kernel,
        out_shape=(jax.ShapeDtypeStruct((B, S, nret), BF16),
                   jax.ShapeDtypeStruct((B, S, 2 * LANES), F32),
                   jax.ShapeDtypeStruct((B, H, HEAD_PAD, S), BF16),
                   jax.ShapeDtypeStruct((B, H, S, HEAD_PAD), BF16),
                   jax.ShapeDtypeStruct((B, H, S // tm, MLA_V_DIM, tm), BF16)),
        grid=(B, S // tm),
        in_specs=[pl.BlockSpec((1, tm, 1), lambda b, s: (b, s, 0)), *specs[:4],
                  pl.BlockSpec((1, tm, D), lambda b, s: (b, s, 0)), *specs[4:]],
        out_specs=(pl.BlockSpec((1, tm, nret), lambda b, s: (b, s, 0)),
                   pl.BlockSpec((1, tm, 2 * LANES), lambda b, s: (b, s, 0)),
                   pl.BlockSpec((1, H, HEAD_PAD, tm), lambda b, s: (b, 0, 0, s)),
                   pl.BlockSpec((1, H, tm, HEAD_PAD), lambda b, s: (b, 0, s, 0)),
                   pl.BlockSpec((1, H, 1, MLA_V_DIM, tm), lambda b, s: (b, 0, s, 0, 0))),
        compiler_params=pltpu.CompilerParams(
            dimension_semantics=("parallel", "parallel"), vmem_limit_bytes=VMEM_LIMIT),
        name="front",
    )(pos, *rope, x, g1, win, gqa, wqt, gkva, wk, wvt, gqc, gqsc, gk, gks)


def _log_sigmoid(x):
    return jnp.minimum(x, 0.0) - jnp.log(1.0 + jnp.exp(-jnp.abs(x)))


def _retention_kernel(q_ref, k_ref, v_ref, g_ref, cos_ref, sin_ref,
                      lfl_ref, lbl_ref, lfc_ref, lbc_ref, o_ref, qs, ks, kv, st, osc):
    S = q_ref.shape[1]
    C = RET_CHUNK
    n_chunks = S // C
    HD = RET_HEAD_DIM

    lane = lax.broadcasted_iota(jnp.int32, (1, LANES), 1)
    head_a = lane < HD

    r2 = lax.broadcasted_iota(jnp.int32, (LANES, LANES), 0)
    c2 = lax.broadcasted_iota(jnp.int32, (LANES, LANES), 1)
    rw = lax.broadcasted_iota(jnp.int32, (2 * LANES, 2 * LANES), 0)
    cw = lax.broadcasted_iota(jnp.int32, (2 * LANES, 2 * LANES), 1)
    partner = jnp.where((cw % HD) < (HD // 2), cw + HD // 2, cw - HD // 2)
    swap2 = (rw == partner).astype(BF16)

    qk = jnp.concatenate([q_ref[0], k_ref[0]], axis=1)
    qk_sw = _dot(qk, swap2)
    cos, sin = cos_ref[0], sin_ref[0]
    qs[...] = (qk[:, :LANES].astype(F32) * cos + qk_sw[:, :LANES] * sin).astype(BF16)
    ks[...] = (qk[:, LANES:].astype(F32) * cos + qk_sw[:, LANES:] * sin).astype(BF16)
    k_scale = HD ** -0.5

    lgf_l = _log_sigmoid(lfl_ref[0])
    lgb_l = _log_sigmoid(lbl_ref[0])
    lgf_c = _log_sigmoid(lfc_ref[0])
    lgb_c = _log_sigmoid(lbc_ref[0])

    row = lax.broadcasted_iota(jnp.int32, (C, 2 * C), 0)
    col = lax.broadcasted_iota(jnp.int32, (C, 2 * C), 1) % C
    diff = (row - col).astype(F32)
    decay = k_scale * jnp.where(diff >= 0, jnp.exp(lgf_c * jnp.maximum(diff, 0.0)),
                                jnp.exp(lgb_c * jnp.maximum(-diff, 0.0)))

    idx = lax.broadcasted_iota(jnp.int32, (C, LANES), 0).astype(F32)
    zeta_f = k_scale * jnp.exp(lgf_l * (C - 1 - idx))
    xi_f = jnp.exp(lgf_l * (idx + 1))
    zeta_b = k_scale * jnp.exp(lgb_l * idx)
    xi_b = jnp.exp(lgb_l * (C - idx))
    gc_f = jnp.exp(lgf_l * C)
    gc_b = jnp.exp(lgb_l * C)

    same_head = (r2 < HD) == (c2 < HD)
    avg = jnp.where(same_head, 1.0 / HD, 0.0).astype(BF16)
    avg2 = jnp.concatenate([avg, avg], axis=0)

    def chunk_rows(i):
        return pl.ds(pl.multiple_of(i * C, C), C)

    zero = jnp.zeros((), BF16)
    c3 = lax.broadcasted_iota(jnp.int32, (LANES, 2 * LANES), 1) % LANES
    r3 = lax.broadcasted_iota(jnp.int32, (LANES, 2 * LANES), 0)
    same_head2 = (r3 < HD) == (c3 < HD)

    def kv_body(i, carry):
        sl = chunk_rows(i)
        vc = v_ref[0, sl, :].astype(F32)
        vz = jnp.concatenate([vc * zeta_f, vc * zeta_b], axis=1).astype(BF16)
        kv[i] = jnp.where(same_head2, _dot_tn(ks[sl, :], vz), 0.0)
        return carry

    unroll = min(RET_UNROLL, n_chunks)
    lax.fori_loop(0, n_chunks, kv_body, 0, unroll=unroll)

    def scan_body(j, carry):
        sf, sb = carry
        jb = n_chunks - 1 - j
        st[j, :LANES, :] = sf.astype(BF16)
        st[jb, LANES:, :] = sb.astype(BF16)
        return sf * gc_f + kv[j, :, :LANES], sb * gc_b + kv[jb, :, LANES:]

    zeros = jnp.zeros((LANES, LANES), F32)
    lax.fori_loop(0, n_chunks, scan_body, (zeros, zeros), unroll=2)

    def out_body(i, carry):
        sl = chunk_rows(i)
        qc = qs[sl, :]
        kc = ks[sl, :]
        vc = v_ref[0, sl, :]
        k2 = jnp.concatenate([jnp.where(head_a, kc, zero), jnp.where(head_a, zero, kc)], axis=0)
        v2 = jnp.concatenate([jnp.where(head_a, vc, zero), jnp.where(head_a, zero, vc)], axis=0)
        p = (_dot_nt(qc, k2) * decay).astype(BF16)
        qf = qc.astype(F32)
        qx = jnp.concatenate([qf * xi_f, qf * xi_b], axis=1).astype(BF16)
        osc[sl, :] = _dot(p, v2) + _dot(qx, st[i])
        return carry

    lax.fori_loop(0, n_chunks, out_body, 0, unroll=unroll)

    rows = min(S, GN_ROWS)
    for r in range(S // rows):
        rs = slice(r * rows, (r + 1) * rows)
        o = osc[rs, :]
        o_hi, o_lo = _hi_lo(o)
        mu = _dot(jnp.concatenate([o_hi, o_lo], axis=1), avg2)
        d = o - mu
        var = _dot((d * d).astype(BF16), avg)
        gate = g_ref[0, rs, :].astype(F32)
        o_ref[0, rs, :] = (gate * _sigmoid(gate) * d * lax.rsqrt(var + GN_EPS)).astype(o_ref.dtype)


def _retention(ret, tables, lfl, lbl, lfc, lbc):
    B, S, _ = ret.shape
    n_pairs = RET_WIDTH // LANES
    n_chunks = S // RET_CHUNK

    def col(off):
        return pl.BlockSpec((1, S, LANES), lambda b, hp: (b, 0, off + hp))

    def per_pair(width):
        return pl.BlockSpec((1, 1, width), lambda b, hp: (hp, 0, 0))

    return pl.pallas_call(
        _retention_kernel,
        out_shape=jax.ShapeDtypeStruct((B, S, RET_WIDTH), BF16),
        grid=(B, n_pairs),
        in_specs=[
            col(0), col(n_pairs), col(2 * n_pairs), col(3 * n_pairs),
            pl.BlockSpec((1, S, LANES), lambda b, hp: (b, 0, 0)),
            pl.BlockSpec((1, S, LANES), lambda b, hp: (b, 0, 1)),
            per_pair(LANES), per_pair(LANES), per_pair(2 * RET_CHUNK), per_pair(2 * RET_CHUNK),
        ],
        out_specs=pl.BlockSpec((1, S, LANES), lambda b, hp: (b, 0, hp)),
        scratch_shapes=[
            pltpu.VMEM((S, LANES), BF16),
            pltpu.VMEM((S, LANES), BF16),
            pltpu.VMEM((n_chunks, LANES, 2 * LANES), F32),
            pltpu.VMEM((n_chunks, 2 * LANES, LANES), BF16),
            pltpu.VMEM((S, LANES), F32),
        ],
        compiler_params=pltpu.CompilerParams(
            dimension_semantics=("parallel", "parallel"), vmem_limit_bytes=VMEM_LIMIT),
        name="retention",
    )(ret, ret, ret, ret, tables, tables, lfl, lbl, lfc, lbc)


def _attention_kernel(qt_ref, k_ref, vt_ref, o_ref, s0, s1, m0, m1, acc_scr):
    i = pl.program_id(0)
    tk = vt_ref.shape[4]
    n_sub, S, tqs = s0.shape
    kb = min(ATT_KEY_ROWS, tk)

    @pl.when(i == 0)
    def _():
        s1[...] = jnp.zeros(s1.shape, F32)
        m1[...] = jnp.zeros(m1.shape, F32)

    def step(s_cur, m_cur, s_prev, m_prev_ref):
        for sub in range(n_sub):
            cols = slice(sub * tqs, (sub + 1) * tqs)
            qt = qt_ref[0, 0, :, cols]
            m_prev = m_prev_ref[sub]
            acc_scr[sub] = jnp.zeros(acc_scr.shape[1:], F32)
            m8 = jnp.full((8, tqs), -jnp.inf, F32)
            l8 = jnp.zeros((8, tqs), F32)
            for kr in range(S // kb):
                rows = slice(kr * kb, (kr + 1) * kb)
                s = _dot(k_ref[0, 0, rows, :], qt)
                s_cur[sub, rows, :] = s
                m8 = jnp.maximum(m8, jnp.max(s.reshape(kb // 8, 8, tqs), axis=0))
                p = jnp.exp2(s_prev[sub, rows, :] - m_prev)
                l8 = l8 + jnp.sum(p.reshape(kb // 8, 8, tqs), axis=0)
                kt, off = divmod(kr * kb, tk)
                acc_scr[sub] += _dot(vt_ref[0, 0, kt, :, off:off + kb], p.astype(BF16))
            m_cur[sub] = jnp.max(m8, axis=0, keepdims=True)
            denom = jnp.sum(l8, axis=0, keepdims=True)
            o_ref[0, :, cols] = (acc_scr[sub] / denom).astype(o_ref.dtype)

    @pl.when(i % 2 == 0)
    def _():
        step(s0, m0, s1, m1)

    @pl.when(i % 2 == 1)
    def _():
        step(s1, m1, s0, m0)


def _attention(qt, k, vt, tq):
    B, H, _, S = qt.shape
    n_k, _, tk = vt.shape[2:]
    n_q = S // tq
    n_tiles = B * H * n_q
    tqs = min(tq, ATT_SUB_TILE)
    n_sub = tq // tqs

    def tile(t):
        return t // (H * n_q), (t // n_q) % H, t % n_q

    def score_tile(i):
        return tile(jnp.minimum(i, n_tiles - 1))

    def value_tile(i):
        return tile(jnp.maximum(i - 1, 0))

    def qt_map(i):
        b, h, q = score_tile(i)
        return b, h, 0, q

    def k_map(i):
        b, h, _ = score_tile(i)
        return b, h, 0, 0

    def vt_map(i):
        b, h, _ = value_tile(i)
        return b, h, 0, 0, 0

    return pl.pallas_call(
        _attention_kernel,
        out_shape=jax.ShapeDtypeStruct((B, H * MLA_V_DIM, S), BF16),
        grid=(n_tiles + 1,),
        in_specs=[
            pl.BlockSpec((1, 1, HEAD_PAD, tq), qt_map),
            pl.BlockSpec((1, 1, S, HEAD_PAD), k_map),
            pl.BlockSpec((1, 1, n_k, MLA_V_DIM, tk), vt_map),
        ],
        out_specs=pl.BlockSpec((1, MLA_V_DIM, tq), value_tile),
        scratch_shapes=[
            pltpu.VMEM((n_sub, S, tqs), F32), pltpu.VMEM((n_sub, S, tqs), F32),
            pltpu.VMEM((n_sub, 1, tqs), F32), pltpu.VMEM((n_sub, 1, tqs), F32),
            pltpu.VMEM((n_sub, MLA_V_DIM, tqs), F32),
        ],
        compiler_params=pltpu.CompilerParams(
            dimension_semantics=("arbitrary",), vmem_limit_bytes=VMEM_LIMIT),
        name="attention",
    )(qt, k, vt)


def _out_ffn_kernel(x_ref, yr_ref, yt_ref, wo1_ref, wo2_ref, g2_ref, wg_ref, wu_ref, wd_ref, o_ref):
    x = x_ref[0]
    attn = _dot(yr_ref[0], wo1_ref[...]) + _dot_tn(yt_ref[0], wo2_ref[...])
    x1 = x + attn
    h = _rms(x1, g2_ref[...], x1.shape[-1]).astype(BF16)
    gate = _dot(h, wg_ref[...])
    up = _dot(h, wu_ref[...])
    a = (gate * _sigmoid(gate) * up).astype(BF16)
    o_ref[0] = x1 + _dot(a, wd_ref[...])


def _out_ffn(x, y_ret, y_mla_t, wo1, wo2, g2, wg, wu, wd, tm):
    B, S, D = x.shape

    def const(a):
        return pl.BlockSpec(a.shape, lambda b, s: (0,) * a.ndim, pipeline_mode=pl.Buffered(1))

    return pl.pallas_call(
        _out_ffn_kernel,
        out_shape=jax.ShapeDtypeStruct((B, S, D), F32),
        grid=(B, S // tm),
        in_specs=[
            pl.BlockSpec((1, tm, D), lambda b, s: (b, s, 0)),
            pl.BlockSpec((1, tm, RET_WIDTH), lambda b, s: (b, s, 0)),
            pl.BlockSpec((1, MLA_WIDTH, tm), lambda b, s: (b, 0, s)),
            const(wo1), const(wo2), const(g2), const(wg), const(wu), const(wd),
        ],
        out_specs=pl.BlockSpec((1, tm, D), lambda b, s: (b, s, 0)),
        compiler_params=pltpu.CompilerParams(
            dimension_semantics=("parallel", "parallel"), vmem_limit_bytes=VMEM_LIMIT),
        name="out_ffn",
    )(x, y_ret, y_mla_t, wo1, wo2, g2, wg, wu, wd)


def _rotate_half_cols(w):
    half = w.shape[-1] // 2
    return jnp.concatenate([-w[..., half:], w[..., :half]], axis=-1)


def _swap_halves(g):
    half = g.shape[-1] // 2
    return jnp.concatenate([g[..., half:], g[..., :half]], axis=-1)


def _rope_layout():
    hr = RET_HEAD_DIM // 2
    hm = MLA_ROPE_DIM // 2
    fr = ROPE_BASE ** (-jnp.arange(hr, dtype=F32) / hr)
    fm = ROPE_BASE ** (-jnp.arange(hm, dtype=F32) / hm)
    zeros = jnp.zeros((LANES - 2 * hr - 2 * hm,), F32)
    freq = jnp.concatenate([fr, fr, fm, fm, zeros])[None]
    half_pi = jnp.full((1,), math.pi / 2, F32)
    phase = jnp.concatenate([jnp.tile(half_pi, hr), jnp.zeros((hr,), F32),
                             jnp.tile(half_pi, hm), jnp.zeros((hm,), F32), zeros])[None]

    l = jnp.arange(LANES)
    src = jnp.arange(LANES)[:, None]
    ret_cos = (src == (l % hr)[None]).astype(F32)
    ret_sign = jnp.where((l % RET_HEAD_DIM) < hr, -1.0, 1.0)
    ret_sin = (src == (hr + l % hr)[None]).astype(F32) * ret_sign[None]
    on_rope = (l >= MLA_NOPE_DIM) & (l < MLA_QK_DIM)
    j = (l - MLA_NOPE_DIM) % hm
    mla_cos = ((src == (2 * hr + j)[None]) & on_rope[None]).astype(F32)
    mla_sin = ((src == (2 * hr + hm + j)[None]) & on_rope[None]).astype(F32)
    expand = jnp.concatenate([ret_cos, ret_sin, mla_cos, mla_sin], axis=1).astype(BF16)
    bias = jnp.concatenate([jnp.zeros((2 * LANES,), F32), jnp.where(on_rope, 0.0, 1.0),
                            jnp.zeros((LANES,), F32)])[None]
    return freq, phase, expand, bias


def _pad_lanes(a, width):
    return jnp.pad(a, [(0, 0)] * (a.ndim - 1) + [(0, width - a.shape[-1])])


def kernel(x, positions, norm1_g, w_in, ret_decay_logit_fwd, ret_decay_logit_bwd, q_a_norm_g, w_uq,
           kv_a_norm_g, w_ukv, q_norm_g, k_norm_g, w_o, norm2_g, w_gate, w_up, w_down):
    B, S, D = x.shape
    depth = w_in.shape[0]
    H = MLA_HEADS
    tm = min(512, S)
    tq = min(1024, S)

    rope = _rope_layout()
    pos = positions.astype(F32)[..., None]

    for layer in range(depth):
        wi = w_in[layer]
        w_rope = wi[:, -MLA_ROPE_DIM:]
        n_in = 4 * RET_WIDTH + 4 * LANES
        wi = _pad_lanes(jnp.concatenate([wi, _rotate_half_cols(w_rope)], axis=1), n_in).astype(BF16)

        wq = w_uq[layer].reshape(MLA_Q_RANK, H, MLA_QK_DIM)
        wqt = _pad_lanes(wq, HEAD_PAD).reshape(MLA_Q_RANK, H * HEAD_PAD).T.astype(BF16)
        wkv = w_ukv[layer].reshape(MLA_KV_RANK, H, MLA_NOPE_DIM + MLA_V_DIM)
        wk = _pad_lanes(wkv[..., :MLA_NOPE_DIM], HEAD_PAD).reshape(MLA_KV_RANK, H * HEAD_PAD).astype(BF16)
        wvt = wkv[..., MLA_NOPE_DIM:].reshape(MLA_KV_RANK, H * MLA_V_DIM).T.astype(BF16)

        def head_gain(g):
            g_sw = jnp.concatenate([jnp.zeros((MLA_NOPE_DIM,), F32), _swap_halves(g[MLA_NOPE_DIM:])])
            return _pad_lanes(g, HEAD_PAD)[None], _pad_lanes(g_sw, HEAD_PAD)[None]

        gq, gqs = head_gain(q_norm_g[layer])
        gk, gks = head_gain(k_norm_g[layer])

        def per_pair(logit, reps):
            return jnp.repeat(logit.reshape(RET_HEADS // 2, 1, 2), reps, axis=-1)

        lf, lb = ret_decay_logit_fwd[layer], ret_decay_logit_bwd[layer]

        ret, tables, qt, k, vt = _front(pos, rope, x, norm1_g[layer][None], wi, q_a_norm_g[layer][None], wqt,
                                        kv_a_norm_g[layer][None], wk, wvt, gq.T, gqs.T, gk, gks, tm)
        y_ret = _retention(ret, tables, per_pair(lf, RET_HEAD_DIM), per_pair(lb, RET_HEAD_DIM),
                           per_pair(lf, RET_CHUNK), per_pair(lb, RET_CHUNK))
        y_mla_t = _attention(qt, k, vt, tq)
        wo = w_o[layer].astype(BF16)
        x = _out_ffn(x, y_ret, y_mla_t, wo[:RET_WIDTH], wo[RET_WIDTH:], norm2_g[layer][None],
                     w_gate[layer].astype(BF16), w_up[layer].astype(BF16), w_down[layer].astype(BF16), tm)
    return x
```

```python
import math

import jax
import jax.numpy as jnp
from jax import lax
from jax.experimental import pallas as pl
from jax.experimental.pallas import tpu as pltpu

RET_HEADS = 8
RET_HEAD_DIM = 64
RET_WIDTH = RET_HEADS * RET_HEAD_DIM
RET_CHUNK = 128
MLA_HEADS = 8
MLA_Q_RANK = 256
MLA_KV_RANK = 128
MLA_NOPE_DIM = 64
MLA_ROPE_DIM = 32
MLA_QK_DIM = MLA_NOPE_DIM + MLA_ROPE_DIM
MLA_V_DIM = 64
MLA_WIDTH = MLA_HEADS * MLA_V_DIM
ROPE_BASE = 10000.0
EPS = 1e-6
GN_EPS = 1e-5

LANES = 128
HEAD_PAD = LANES
RET_UNROLL = 32
GN_ROWS = 4096
ATT_SUB_TILE = 512
ATT_KEY_ROWS = 256
FFN_CHUNK = 256
FRONT_ROWS = 128
VMEM_LIMIT = 56 * 1024 * 1024

F32 = jnp.float32
BF16 = jnp.bfloat16


def _dot(a, b):
    return jnp.dot(a, b, preferred_element_type=F32)


def _dot_nt(a, b):
    return lax.dot_general(a, b, (((1,), (1,)), ((), ())), preferred_element_type=F32)


def _dot_tn(a, b):
    return lax.dot_general(a, b, (((0,), (0,)), ((), ())), preferred_element_type=F32)


def _rms(x, g, n):
    ms = jnp.sum(x * x, axis=-1, keepdims=True) * (1.0 / n)
    return x * lax.rsqrt(ms + EPS) * g


def _sigmoid(x):
    return 1.0 / (1.0 + jnp.exp(-x))


def _hi_lo(x):
    hi = x.astype(BF16)
    lo = (x - hi.astype(F32)).astype(BF16)
    return hi, lo


def _front_kernel(pos_ref, freq_ref, phase_ref, expand_ref, bias_ref, x_ref, g1_ref, win_ref,
                  gqa_ref, wqt_ref, gkva_ref, wk_ref, wvt_ref, gqc_ref, gqsc_ref, gk_ref, gks_ref,
                  ret_ref, tab_ref, qt_ref, k_ref, vt_ref):
    H = MLA_HEADS
    tm = x_ref.shape[1]
    nret = ret_ref.shape[-1]
    lane = lax.broadcasted_iota(jnp.int32, (1, LANES), 1)
    rope_lanes = (lane >= MLA_NOPE_DIM) & (lane < MLA_QK_DIM)
    scale = MLA_QK_DIM ** -0.5 * math.log2(math.e)

    def tables(rows):
        tab = jnp.sin(pos_ref[0, rows, :] * freq_ref[...] + phase_ref[...])
        hi, lo = _hi_lo(tab)
        e = expand_ref[...]
        full = _dot(hi, e) + _dot(lo, e) + bias_ref[...]
        tab_ref[0, rows, :] = full[:, :2 * LANES]
        return full[:, 2 * LANES:]

    def project(rows):
        x = x_ref[0, rows, :]
        h = _rms(x, g1_ref[...], x.shape[-1]).astype(BF16)
        proj = _dot(h, win_ref[...])
        ret_ref[0, rows, :] = proj[:, :nret].astype(BF16)
        return proj[:, nret:]

    def mla_prep(rows, mtab, m):
        cq = m[:, :MLA_Q_RANK]
        ckv = m[:, MLA_Q_RANK:MLA_Q_RANK + MLA_KV_RANK]
        kr = m[:, MLA_Q_RANK + MLA_KV_RANK:]
        cosm = mtab[:, :LANES]
        sinm = mtab[:, LANES:]
        tab_t = mtab.T

        cq_t = _rms(cq, gqa_ref[...], MLA_Q_RANK).T.astype(BF16)
        q2t = _dot(wqt_ref[...], cq_t)
        q_cos = gqc_ref[...] * tab_t[:HEAD_PAD]
        q_sin = gqsc_ref[...] * tab_t[HEAD_PAD:]
        r0, r1, r2 = MLA_NOPE_DIM, MLA_NOPE_DIM + MLA_ROPE_DIM // 2, MLA_QK_DIM
        for hd in range(H):
            qh = q2t[hd * HEAD_PAD:(hd + 1) * HEAD_PAD]
            qsw = jnp.concatenate([qh[:r0], -qh[r1:r2], qh[r0:r1], qh[r2:]], axis=0)
            ss = jnp.sum(qh * qh, axis=0, keepdims=True)
            r = lax.rsqrt(ss * (1.0 / MLA_QK_DIM) + EPS) * scale
            qt_ref[0, hd, :, rows] = ((qh * q_cos + qsw * q_sin) * r).astype(BF16)

        ckv_n = _rms(ckv, gkva_ref[...], MLA_KV_RANK)
        kn_all = _dot(ckv_n.astype(BF16), wk_ref[...])
        kr_sq = jnp.where(lane < MLA_ROPE_DIM, kr * kr, 0.0)
        ss_rope = jnp.sum(kr_sq, axis=-1, keepdims=True)
        k_rope = jnp.where(rope_lanes, pltpu.roll(kr, MLA_NOPE_DIM, 1), 0.0)
        k_rope_sw = jnp.where(rope_lanes, pltpu.roll(kr, MLA_NOPE_DIM - MLA_ROPE_DIM, 1), 0.0)
        k_cos = gk_ref[...] * cosm
        rope_term = k_rope * k_cos + k_rope_sw * (gks_ref[...] * sinm)
        for hd in range(H):
            kn = kn_all[:, hd * HEAD_PAD:(hd + 1) * HEAD_PAD]
            ss = jnp.sum(kn * kn, axis=-1, keepdims=True) + ss_rope
            r = lax.rsqrt(ss * (1.0 / MLA_QK_DIM) + EPS)
            k_ref[0, hd, rows, :] = ((kn * k_cos + rope_term) * r).astype(BF16)

        vt = _dot(wvt_ref[...], ckv_n.T.astype(BF16))
        for hd in range(H):
            vt_ref[0, hd, 0, :, rows] = vt[hd * MLA_V_DIM:(hd + 1) * MLA_V_DIM].astype(BF16)

    n_blk = tm // FRONT_ROWS
    blocks = [slice(i * FRONT_ROWS, (i + 1) * FRONT_ROWS) for i in range(n_blk)]
    ms = [project(rows) for rows in blocks]
    mtabs = [tables(rows) for rows in blocks]
    for rows, mtab, m in zip(blocks, mtabs, ms):
        mla_prep(rows, mtab, m)


def _front(pos, rope, x, g1, win, gqa, wqt, gkva, wk, wvt, gqc, gqsc, gk, gks, tm):
    B, S, D = x.shape
    H = MLA_HEADS
    nret = 4 * RET_WIDTH
    consts = (*rope, g1, win, gqa, wqt, gkva, wk, wvt, gqc, gqsc, gk, gks)

    def const(a):
        return pl.BlockSpec(a.shape, lambda b, s: (0,) * a.ndim, pipeline_mode=pl.Buffered(1))

    specs = [const(a) for a in consts]
    return pl.pallas_call(
        _front_kernel,
        out_shape=(jax.ShapeDtypeStruct((B, S, nret), BF16),
                   jax.ShapeDtypeStruct((B, S, 2 * LANES), F32),
                   jax.ShapeDtypeStruct((B, H, HEAD_PAD, S), BF16),
                   jax.ShapeDtypeStruct((B, H, S, HEAD_PAD), BF16),
                   jax.ShapeDtypeStruct((B, H, S // tm, MLA_V_DIM, tm), BF16)),
        grid=(B, S // tm),
        in_specs=[pl.BlockSpec((1, tm, 1), lambda b, s: (b, s, 0)), *specs[:4],
                  pl.BlockSpec((1, tm, D), lambda b, s: (b, s, 0)), *specs[4:]],
        out_specs=(pl.BlockSpec((1, tm, nret), lambda b, s: (b, s, 0)),
                   pl.BlockSpec((1, tm, 2 * LANES), lambda b, s: (b, s, 0)),
                   pl.BlockSpec((1, H, HEAD_PAD, tm), lambda b, s: (b, 0, 0, s)),
                   pl.BlockSpec((1, H, tm, HEAD_PAD), lambda b, s: (b, 0, s, 0)),
                   pl.BlockSpec((1, H, 1, MLA_V_DIM, tm), lambda b, s: (b, 0, s, 0, 0))),
        compiler_params=pltpu.CompilerParams(
            dimension_semantics=("parallel", "parallel"), vmem_limit_bytes=VMEM_LIMIT),
        name="front",
    )(pos, *rope, x, g1, win, gqa, wqt, gkva, wk, wvt, gqc, gqsc, gk, gks)


def _log_sigmoid(x):
    return jnp.minimum(x, 0.0) - jnp.log(1.0 + jnp.exp(-jnp.abs(x)))


def _retention_kernel(q_ref, k_ref, v_ref, g_ref, cos_ref, sin_ref,
                      lfl_ref, lbl_ref, lfc_ref, lbc_ref, o_ref, qs, ks, kv, st, osc):
    S = q_ref.shape[1]
    C = RET_CHUNK
    n_chunks = S // C
    HD = RET_HEAD_DIM

    lane = lax.broadcasted_iota(jnp.int32, (1, LANES), 1)
    head_a = lane < HD

    r2 = lax.broadcasted_iota(jnp.int32, (LANES, LANES), 0)
    c2 = lax.broadcasted_iota(jnp.int32, (LANES, LANES), 1)
    rw = lax.broadcasted_iota(jnp.int32, (2 * LANES, 2 * LANES), 0)
    cw = lax.broadcasted_iota(jnp.int32, (2 * LANES, 2 * LANES), 1)
    partner = jnp.where((cw % HD) < (HD // 2), cw + HD // 2, cw - HD // 2)
    swap2 = (rw == partner).astype(BF16)

    qk = jnp.concatenate([q_ref[0], k_ref[0]], axis=1)
    qk_sw = _dot(qk, swap2)
    cos, sin = cos_ref[0], sin_ref[0]
    qs[...] = (qk[:, :LANES].astype(F32) * cos + qk_sw[:, :LANES] * sin).astype(BF16)
    ks[...] = (qk[:, LANES:].astype(F32) * cos + qk_sw[:, LANES:] * sin).astype(BF16)
    k_scale = HD ** -0.5

    lgf_l = _log_sigmoid(lfl_ref[0])
    lgb_l = _log_sigmoid(lbl_ref[0])
    lgf_c = _log_sigmoid(lfc_ref[0])
    lgb_c = _log_sigmoid(lbc_ref[0])

    row = lax.broadcasted_iota(jnp.int32, (C, 2 * C), 0)
    col = lax.broadcasted_iota(jnp.int32, (C, 2 * C), 1) % C
    diff = (row - col).astype(F32)
    decay = k_scale * jnp.where(diff >= 0, jnp.exp(lgf_c * jnp.maximum(diff, 0.0)),
                                jnp.exp(lgb_c * jnp.maximum(-diff, 0.0)))

    idx = lax.broadcasted_iota(jnp.int32, (C, LANES), 0).astype(F32)
    zeta_f = k_scale * jnp.exp(lgf_l * (C - 1 - idx))
    xi_f = jnp.exp(lgf_l * (idx + 1))
    zeta_b = k_scale * jnp.exp(lgb_l * idx)
    xi_b = jnp.exp(lgb_l * (C - idx))
    gc_f = jnp.exp(lgf_l * C)
    gc_b = jnp.exp(lgb_l * C)

    same_head = (r2 < HD) == (c2 < HD)
    avg = jnp.where(same_head, 1.0 / HD, 0.0).astype(BF16)
    avg2 = jnp.concatenate([avg, avg], axis=0)

    def chunk_rows(i):
        return pl.ds(pl.multiple_of(i * C, C), C)

    zero = jnp.zeros((), BF16)
    c3 = lax.broadcasted_iota(jnp.int32, (LANES, 2 * LANES), 1) % LANES
    r3 = lax.broadcasted_iota(jnp.int32, (LANES, 2 * LANES), 0)
    same_head2 = (r3 < HD) == (c3 < HD)

    def kv_body(i, carry):
        sl = chunk_rows(i)
        vc = v_ref[0, sl, :].astype(F32)
        vz = jnp.concatenate([vc * zeta_f, vc * zeta_b], axis=1).astype(BF16)
        kv[i] = jnp.where(same_head2, _dot_tn(ks[sl, :], vz), 0.0)
        return carry

    unroll = min(RET_UNROLL, n_chunks)
    lax.fori_loop(0, n_chunks, kv_body, 0, unroll=unroll)

    def scan_body(j, carry):
        sf, sb = carry
        jb = n_chunks - 1 - j
        st[j, :LANES, :] = sf.astype(BF16)
        st[jb, LANES:, :] = sb.astype(BF16)
        return sf * gc_f + kv[j, :, :LANES], sb * gc_b + kv[jb, :, LANES:]

    zeros = jnp.zeros((LANES, LANES), F32)
    lax.fori_loop(0, n_chunks, scan_body, (zeros, zeros), unroll=2)

    def out_body(i, carry):
        sl = chunk_rows(i)
        qc = qs[sl, :]
        kc = ks[sl, :]
        vc = v_ref[0, sl, :]
        k2 = jnp.concatenate([jnp.where(head_a, kc, zero), jnp.where(head_a, zero, kc)], axis=0)
        v2 = jnp.concatenate([jnp.where(head_a, vc, zero), jnp.where(head_a, zero, vc)], axis=0)
        p = (_dot_nt(qc, k2) * decay).astype(BF16)
        qf = qc.astype(F32)
        qx = jnp.concatenate([qf * xi_f, qf * xi_b], axis=1).astype(BF16)
        osc[sl, :] = _dot(p, v2) + _dot(qx, st[i])
        return carry

    lax.fori_loop(0, n_chunks, out_body, 0, unroll=unroll)

    rows = min(S, GN_ROWS)
    for r in range(S // rows):
        rs = slice(r * rows, (r + 1) * rows)
        o = osc[rs, :]
        o_hi, o_lo = _hi_lo(o)
        mu = _dot(jnp.concatenate([o_hi, o_lo], axis=1), avg2)
        d = o - mu
        var = _dot((d * d).astype(BF16), avg)
        gate = g_ref[0, rs, :].astype(F32)
        o_ref[0, rs, :] = (gate * _sigmoid(gate) * d * lax.rsqrt(var + GN_EPS)).astype(o_ref.dtype)


def _retention(ret, tables, lfl, lbl, lfc, lbc):
    B, S, _ = ret.shape
    n_pairs = RET_WIDTH // LANES
    n_chunks = S // RET_CHUNK

    def col(off):
        return pl.BlockSpec((1, S, LANES), lambda b, hp: (b, 0, off + hp))

    def per_pair(width):
        return pl.BlockSpec((1, 1, width), lambda b, hp: (hp, 0, 0))

    return pl.pallas_call(
        _retention_kernel,
        out_shape=jax.ShapeDtypeStruct((B, S, RET_WIDTH), BF16),
        grid=(B, n_pairs),
        in_specs=[
            col(0), col(n_pairs), col(2 * n_pairs), col(3 * n_pairs),
            pl.BlockSpec((1, S, LANES), lambda b, hp: (b, 0, 0)),
            pl.BlockSpec((1, S, LANES), lambda b, hp: (b, 0, 1)),
            per_pair(LANES), per_pair(LANES), per_pair(2 * RET_CHUNK), per_pair(2 * RET_CHUNK),
        ],
        out_specs=pl.BlockSpec((1, S, LANES), lambda b, hp: (b, 0, hp)),
        scratch_shapes=[
            pltpu.VMEM((S, LANES), BF16),
            pltpu.VMEM((S, LANES), BF16),
            pltpu.VMEM((n_chunks, LANES, 2 * LANES), F32),
            pltpu.VMEM((n_chunks, 2 * LANES, LANES), BF16),
            pltpu.VMEM((S, LANES), F32),
        ],
        compiler_params=pltpu.CompilerParams(
            dimension_semantics=("parallel", "parallel"), vmem_limit_bytes=VMEM_LIMIT),
        name="retention",
    )(ret, ret, ret, ret, tables, tables, lfl, lbl, lfc, lbc)


def _attention_kernel(qt_ref, k_ref, vt_ref, o_ref, s0, s1, m0, m1, acc_scr):
    i = pl.program_id(0)
    tk = vt_ref.shape[4]
    n_sub, S, tqs = s0.shape
    kb = min(ATT_KEY_ROWS, tk)

    @pl.when(i == 0)
    def _():
        s1[...] = jnp.zeros(s1.shape, F32)
        m1[...] = jnp.zeros(m1.shape, F32)

    def step(s_cur, m_cur, s_prev, m_prev_ref):
        for sub in range(n_sub):
            cols = slice(sub * tqs, (sub + 1) * tqs)
            qt = qt_ref[0, 0, :, cols]
            m_prev = m_prev_ref[sub]
            acc_scr[sub] = jnp.zeros(acc_scr.shape[1:], F32)
            m8 = jnp.full((8, tqs), -jnp.inf, F32)
            l8 = jnp.zeros((8, tqs), F32)
            for kr in range(S // kb):
                rows = slice(kr * kb, (kr + 1) * kb)
                s = _dot(k_ref[0, 0, rows, :], qt)
                s_cur[sub, rows, :] = s
                m8 = jnp.maximum(m8, jnp.max(s.reshape(kb // 8, 8, tqs), axis=0))
                p = jnp.exp2(s_prev[sub, rows, :] - m_prev)
                l8 = l8 + jnp.sum(p.reshape(kb // 8, 8, tqs), axis=0)
                kt, off = divmod(kr * kb, tk)
                acc_scr[sub] += _dot(vt_ref[0, 0, kt, :, off:off + kb], p.astype(BF16))
            m_cur[sub] = jnp.max(m8, axis=0, keepdims=True)
            denom = jnp.sum(l8, axis=0, keepdims=True)
            o_ref[0, :, cols] = (acc_scr[sub] / denom).astype(o_ref.dtype)

    @pl.when(i % 2 == 0)
    def _():
        step(s0, m0, s1, m1)

    @pl.when(i % 2 == 1)
    def _():
        step(s1, m1, s0, m0)


def _attention(qt, k, vt, tq):
    B, H, _, S = qt.shape
    n_k, _, tk = vt.shape[2:]
    n_q = S // tq
    n_tiles = B * H * n_q
    tqs = min(tq, ATT_SUB_TILE)
    n_sub = tq // tqs

    def tile(t):
        return t // (H * n_q), (t // n_q) % H, t % n_q

    def score_tile(i):
        return tile(jnp.minimum(i, n_tiles - 1))

    def value_tile(i):
        return tile(jnp.maximum(i - 1, 0))

    def qt_map(i):
        b, h, q = score_tile(i)
        return b, h, 0, q

    def k_map(i):
        b, h, _ = score_tile(i)
        return b, h, 0, 0

    def vt_map(i):
        b, h, _ = value_tile(i)
        return b, h, 0, 0, 0

    return pl.pallas_call(
        _attention_kernel,
        out_shape=jax.ShapeDtypeStruct((B, H * MLA_V_DIM, S), BF16),
        grid=(n_tiles + 1,),
        in_specs=[
            pl.BlockSpec((1, 1, HEAD_PAD, tq), qt_map),
            pl.BlockSpec((1, 1, S, HEAD_PAD), k_map),
            pl.BlockSpec((1, 1, n_k, MLA_V_DIM, tk), vt_map),
        ],
        out_specs=pl.BlockSpec((1, MLA_V_DIM, tq), value_tile),
        scratch_shapes=[
            pltpu.VMEM((n_sub, S, tqs), F32), pltpu.VMEM((n_sub, S, tqs), F32),
            pltpu.VMEM((n_sub, 1, tqs), F32), pltpu.VMEM((n_sub, 1, tqs), F32),
            pltpu.VMEM((n_sub, MLA_V_DIM, tqs), F32),
        ],
        compiler_params=pltpu.CompilerParams(
            dimension_semantics=("arbitrary",), vmem_limit_bytes=VMEM_LIMIT),
        name="attention",
    )(qt, k, vt)


def _out_ffn_kernel(x_ref, yr_ref, yt_ref, wo1_ref, wo2_ref, g2_ref, wg_ref, wu_ref, wd_ref, o_ref):
    x = x_ref[0]
    attn = _dot(yr_ref[0], wo1_ref[...]) + _dot_tn(yt_ref[0], wo2_ref[...])
    x1 = x + attn
    h = _rms(x1, g2_ref[...], x1.shape[-1]).astype(BF16)
    acc = x1
    for c in range(wg_ref.shape[1] // FFN_CHUNK):
        cols = slice(c * FFN_CHUNK, (c + 1) * FFN_CHUNK)
        gate = _dot(h, wg_ref[:, cols])
        up = _dot(h, wu_ref[:, cols])
        a = (gate * _sigmoid(gate) * up).astype(BF16)
        acc = acc + _dot(a, wd_ref[cols, :])
    o_ref[0] = acc


def _out_ffn(x, y_ret, y_mla_t, wo1, wo2, g2, wg, wu, wd, tm):
    B, S, D = x.shape

    def const(a):
        return pl.BlockSpec(a.shape, lambda b, s: (0,) * a.ndim, pipeline_mode=pl.Buffered(1))

    return pl.pallas_call(
        _out_ffn_kernel,
        out_shape=jax.ShapeDtypeStruct((B, S, D), F32),
        grid=(B, S // tm),
        in_specs=[
            pl.BlockSpec((1, tm, D), lambda b, s: (b, s, 0)),
            pl.BlockSpec((1, tm, RET_WIDTH), lambda b, s: (b, s, 0)),
            pl.BlockSpec((1, MLA_WIDTH, tm), lambda b, s: (b, 0, s)),
            const(wo1), const(wo2), const(g2), const(wg), const(wu), const(wd),
        ],
        out_specs=pl.BlockSpec((1, tm, D), lambda b, s: (b, s, 0)),
        compiler_params=pltpu.CompilerParams(
            dimension_semantics=("parallel", "parallel"), vmem_limit_bytes=VMEM_LIMIT),
        name="out_ffn",
    )(x, y_ret, y_mla_t, wo1, wo2, g2, wg, wu, wd)


def _rotate_half_cols(w):
    half = w.shape[-1] // 2
    return jnp.concatenate([-w[..., half:], w[..., :half]], axis=-1)


def _swap_halves(g):
    half = g.shape[-1] // 2
    return jnp.concatenate([g[..., half:], g[..., :half]], axis=-1)


def _rope_layout():
    hr = RET_HEAD_DIM // 2
    hm = MLA_ROPE_DIM // 2
    fr = ROPE_BASE ** (-jnp.arange(hr, dtype=F32) / hr)
    fm = ROPE_BASE ** (-jnp.arange(hm, dtype=F32) / hm)
    zeros = jnp.zeros((LANES - 2 * hr - 2 * hm,), F32)
    freq = jnp.concatenate([fr, fr, fm, fm, zeros])[None]
    half_pi = jnp.full((1,), math.pi / 2, F32)
    phase = jnp.concatenate([jnp.tile(half_pi, hr), jnp.zeros((hr,), F32),
                             jnp.tile(half_pi, hm), jnp.zeros((hm,), F32), zeros])[None]

    l = jnp.arange(LANES)
    src = jnp.arange(LANES)[:, None]
    ret_cos = (src == (l % hr)[None]).astype(F32)
    ret_sign = jnp.where((l % RET_HEAD_DIM) < hr, -1.0, 1.0)
    ret_sin = (src == (hr + l % hr)[None]).astype(F32) * ret_sign[None]
    on_rope = (l >= MLA_NOPE_DIM) & (l < MLA_QK_DIM)
    j = (l - MLA_NOPE_DIM) % hm
    mla_cos = ((src == (2 * hr + j)[None]) & on_rope[None]).astype(F32)
    mla_sin = ((src == (2 * hr + hm + j)[None]) & on_rope[None]).astype(F32)
    expand = jnp.concatenate([ret_cos, ret_sin, mla_cos, mla_sin], axis=1).astype(BF16)
    bias = jnp.concatenate([jnp.zeros((2 * LANES,), F32), jnp.where(on_rope, 0.0, 1.0),
                            jnp.zeros((LANES,), F32)])[None]
    return freq, phase, expand, bias


def _pad_lanes(a, width):
    return jnp.pad(a, [(0, 0)] * (a.ndim - 1) + [(0, width - a.shape[-1])])


def kernel(x, positions, norm1_g, w_in, ret_decay_logit_fwd, ret_decay_logit_bwd, q_a_norm_g, w_uq,
           kv_a_norm_g, w_ukv, q_norm_g, k_norm_g, w_o, norm2_g, w_gate, w_up, w_down):
    B, S, D = x.shape
    depth = w_in.shape[0]
    H = MLA_HEADS
    tm = min(512, S)
    tq = min(1024, S)

    rope = _rope_layout()
    pos = positions.astype(F32)[..., None]

    for layer in range(depth):
        wi = w_in[layer]
        w_rope = wi[:, -MLA_ROPE_DIM:]
        n_in = 4 * RET_WIDTH + 4 * LANES
        wi = _pad_lanes(jnp.concatenate([wi, _rotate_half_cols(w_rope)], axis=1), n_in).astype(BF16)

        wq = w_uq[layer].reshape(MLA_Q_RANK, H, MLA_QK_DIM)
        wqt = _pad_lanes(wq, HEAD_PAD).reshape(MLA_Q_RANK, H * HEAD_PAD).T.astype(BF16)
        wkv = w_ukv[layer].reshape(MLA_KV_RANK, H, MLA_NOPE_DIM + MLA_V_DIM)
        wk = _pad_lanes(wkv[..., :MLA_NOPE_DIM], HEAD_PAD).reshape(MLA_KV_RANK, H * HEAD_PAD).astype(BF16)
        wvt = wkv[..., MLA_NOPE_DIM:].reshape(MLA_KV_RANK, H * MLA_V_DIM).T.astype(BF16)

        def head_gain(g):
            g_sw = jnp.concatenate([jnp.zeros((MLA_NOPE_DIM,), F32), _swap_halves(g[MLA_NOPE_DIM:])])
            return _pad_lanes(g, HEAD_PAD)[None], _pad_lanes(g_sw, HEAD_PAD)[None]

        gq, gqs = head_gain(q_norm_g[layer])
        gk, gks = head_gain(k_norm_g[layer])

        def per_pair(logit, reps):
            return jnp.repeat(logit.reshape(RET_HEADS // 2, 1, 2), reps, axis=-1)

        lf, lb = ret_decay_logit_fwd[layer], ret_decay_logit_bwd[layer]

        ret, tables, qt, k, vt = _front(pos, rope, x, norm1_g[layer][None], wi, q_a_norm_g[layer][None], wqt,
                                        kv_a_norm_g[layer][None], wk, wvt, gq.T, gqs.T, gk, gks, tm)
        y_ret = _retention(ret, tables, per_pair(lf, RET_HEAD_DIM), per_pair(lb, RET_HEAD_DIM),
                           per_pair(lf, RET_CHUNK), per_pair(lb, RET_CHUNK))
        y_mla_t = _attention(qt, k, vt, tq)
        wo = w_o[layer].astype(BF16)
        x = _out_ffn(x, y_ret, y_mla_t, wo[:RET_WIDTH], wo[RET_WIDTH:], norm2_g[layer][None],
                     w_gate[layer].astype(BF16), w_up[layer].astype(BF16), w_down[layer].astype(BF16), tm)
    return x
```
